```python
import math
import jax, jax.numpy as jnp
from jax import lax
import numpy as np

D_MODEL = 1024
BATCH = 16
SEQ = 4096
DEPTH = 2
DEC_BATCH = 8
DEC_SEQ = 64
PAST_LEN = 1024

CHUNK = 64
N_AB = (DEPTH + 1) // 2
N_CD = DEPTH // 2
RMS_EPS = 1e-5
LN_EPS = 1e-5
A_WIDTH = D_MODEL // 2
A_CONV = 3
B_WIDTH = D_MODEL // 2
B_GROUPS = 4
B_GROUP_DIM = B_WIDTH // B_GROUPS
B_CHUNK = 128
C_HEADS = 8
C_HEAD_DIM = 64
C_WIDTH = C_HEADS * C_HEAD_DIM
C_PREV_CHUNKS = 8
C_BAND = C_PREV_CHUNKS * CHUNK
C_KEYS = C_BAND + CHUNK
C_MAX_REL = 128
D_HEADS = 8
D_HEAD_DIM = 64
D_INNER = D_HEADS * D_HEAD_DIM
D_GROUPS = 2
D_STATE = 128
D_CONV = 4
D_XBC = D_INNER + 2 * D_GROUPS * D_STATE
DT_MIN = 0.001
DT_MAX = 0.1
AB_IN = 3 * A_WIDTH + 2 * B_WIDTH
CD_IN = 3 * C_WIDTH + D_INNER + D_XBC + D_HEADS
MIX_WIDTH = A_WIDTH + B_WIDTH
FFN_HIDDEN = -(-(8 * D_MODEL) // (3 * 256)) * 256

kernel_name = 'hybrid_streaming_encoder_step'

F32 = jnp.float32


def rmsnorm(x, g):
    xf = x.astype(F32)
    y = xf * lax.rsqrt(jnp.mean(xf * xf, axis=-1, keepdims=True) + RMS_EPS)
    return (y * g.astype(F32)).astype(x.dtype)


def layernorm(x, g, b):
    xf = x.astype(F32)
    mu = jnp.mean(xf, axis=-1, keepdims=True)
    xc = xf - mu
    var = jnp.mean(xc * xc, axis=-1, keepdims=True)
    return (xc * lax.rsqrt(var + LN_EPS) * g.astype(F32) + b.astype(F32)).astype(x.dtype)


def swiglu(x, wg, wu, wd):
    return (jax.nn.silu(x @ wg) * (x @ wu)) @ wd


def causal_dwconv(u_ext, w):
    width = w.shape[0]
    L = u_ext.shape[1] - (width - 1)
    return sum(w[k] * u_ext[:, k:k + L] for k in range(width))


def mixer_a(xa, gate_b, gate_c, conv_hist, conv_w):
    u = gate_c * xa
    u_ext = jnp.concatenate([conv_hist.astype(u.dtype), u], axis=1)
    y = gate_b * causal_dwconv(u_ext, conv_w)
    return y, u_ext[:, -(A_CONV - 1):]


def mixer_b(u, v, ln_g, ln_b, w_s, b_s):
    b, L, _ = u.shape
    blk = min(L, B_CHUNK)
    n = L // blk
    v = layernorm(v, ln_g, ln_b)
    tri = jnp.tril(jnp.ones((blk, blk), dtype=bool))
    w = jnp.where(tri[None], w_s[:, :blk, :blk], 0)
    vb = v.reshape(b, n, blk, B_GROUPS, B_GROUP_DIM)
    f = jnp.einsum('gts,bnsgc->bntgc', w, vb) + b_s[:, :blk].T[None, None, :, :, None]
    return u * f.reshape(b, L, B_WIDTH), v


def ab_layer(h, conv_hist, w_in, conv_w, ln_g, ln_b, w_s, b_s, w_out):
    proj = h @ w_in
    xa, gate_b, gate_c, u, v = jnp.split(
        proj, [A_WIDTH, 2 * A_WIDTH, 3 * A_WIDTH, 3 * A_WIDTH + B_WIDTH], axis=-1)
    ya, new_hist = mixer_a(xa, gate_b, gate_c, conv_hist, conv_w)
    yb, v_rows = mixer_b(jax.nn.gelu(u, approximate=False), jax.nn.gelu(v, approximate=False),
                         ln_g, ln_b, w_s, b_s)
    return jnp.concatenate([ya, yb], axis=-1) @ w_out, new_hist, v_rows


def rel_bias_gather(table, q_pos, k_pos):
    rel = jnp.clip(q_pos[:, None] - k_pos[None, :], -C_MAX_REL, C_MAX_REL) + C_MAX_REL
    return table[:, rel]


def attn_core(q, k, v, bias, mask):
    s = jnp.einsum('bqhd,bkhd->bhqk', q.astype(F32), k.astype(F32)) * (C_HEAD_DIM ** -0.5)
    s = s + bias.astype(F32)[None]
    if mask is not None:
        s = jnp.where(mask, s, -1e30)
    p = jax.nn.softmax(s, axis=-1)
    return jnp.einsum('bhqk,bkhd->bqhd', p, v.astype(F32)).astype(q.dtype)


def mixer_c_prompt(q, k, v, table):
    b, L, H, dh = q.shape
    nc = L // CHUNK
    pad = ((0, 0), (C_BAND, 0), (0, 0), (0, 0))
    kp = jnp.pad(k, pad)
    vp = jnp.pad(v, pad)
    kj = jnp.arange(C_KEYS)
    bias = rel_bias_gather(table, jnp.arange(CHUNK) + C_BAND, kj)

    def one_chunk(c):
        start = c * CHUNK
        qc = lax.dynamic_slice_in_dim(q, start, CHUNK, axis=1)
        kc = lax.dynamic_slice_in_dim(kp, start, C_KEYS, axis=1)
        vc = lax.dynamic_slice_in_dim(vp, start, C_KEYS, axis=1)
        mask = (start - C_BAND + kj >= 0)[None, :]
        return attn_core(qc, kc, vc, bias, mask)

    out = lax.map(one_chunk, jnp.arange(nc))
    out = jnp.moveaxis(out, 0, 1).reshape(b, L, H * dh)
    keep = min(C_BAND, L)
    return out, k[:, L - keep:], v[:, L - keep:]


def mixer_c_sample(q, k, v, cache_k, cache_v, table):
    b, T, H, dh = q.shape
    Lc = cache_k.shape[1]
    kk = jnp.concatenate([cache_k.astype(k.dtype), k], axis=1)
    vv = jnp.concatenate([cache_v.astype(v.dtype), v], axis=1)
    bias = rel_bias_gather(table, Lc + jnp.arange(T), jnp.arange(Lc + T))
    out = attn_core(q, kk, vv, bias, None)
    return out.reshape(b, T, H * dh)


def ssd(x, dt, A, Bm, Cm, h0, q_len):
    b, L, H, P = x.shape
    nc = L // q_len
    Hg = H // D_GROUPS
    xs = x.astype(F32).reshape(b, nc, q_len, D_GROUPS, Hg, P)
    dts = dt.astype(F32).reshape(b, nc, q_len, D_GROUPS, Hg)
    Bs = Bm.astype(F32).reshape(b, nc, q_len, D_GROUPS, D_STATE)
    Cs = Cm.astype(F32).reshape(b, nc, q_len, D_GROUPS, D_STATE)
    cum = jnp.cumsum(dts * A.astype(F32).reshape(D_GROUPS, Hg), axis=2)
    tri = jnp.tril(jnp.ones((q_len, q_len), dtype=bool))[:, :, None, None]
    seg = cum[:, :, :, None] - cum[:, :, None, :]
    decay = jnp.exp(jnp.where(tri, seg, -jnp.inf))
    xdt = xs * dts[..., None]
    cb = jnp.einsum('bctgn,bcsgn->bctsg', Cs, Bs)
    y_intra = jnp.einsum('bctsgh,bcsghp->bctghp', cb[..., None] * decay, xdt)
    decay_end = jnp.exp(cum[:, :, -1:] - cum)
    states = jnp.einsum('bcsgn,bcsghp->bcghpn', Bs, xdt * decay_end[..., None])
    chunk_decay = jnp.exp(cum[:, :, -1])

    def step(h, inp):
        dec, st = inp
        return dec[..., None, None] * h + st, h

    h_init = h0.astype(F32).reshape(b, D_GROUPS, Hg, P, D_STATE)
    h_last, h_in = lax.scan(step, h_init,
                            (jnp.moveaxis(chunk_decay, 1, 0), jnp.moveaxis(states, 1, 0)))
    h_in = jnp.moveaxis(h_in, 0, 1)
    y_inter = jnp.einsum('bctgn,bcghpn->bctghp', Cs, h_in) * jnp.exp(cum)[..., None]
    y = (y_intra + y_inter).reshape(b, L, H, P)
    return y, h_last.reshape(b, H, P, D_STATE).astype(h0.dtype)


def mixer_d(z, xbc, dt_raw, conv_hist, conv_w, conv_b, dt_bias, a_log, d_skip, norm_g, h0, q_len):
    b, L, _ = xbc.shape
    xbc_ext = jnp.concatenate([conv_hist.astype(xbc.dtype), xbc], axis=1)
    xbc_c = jax.nn.silu(causal_dwconv(xbc_ext, conv_w) + conv_b)
    new_hist = xbc_ext[:, -(D_CONV - 1):]
    gn = D_GROUPS * D_STATE
    xh = xbc_c[..., :D_INNER].reshape(b, L, D_HEADS, D_HEAD_DIM)
    Bm = xbc_c[..., D_INNER:D_INNER + gn].reshape(b, L, D_GROUPS, D_STATE)
    Cm = xbc_c[..., D_INNER + gn:].reshape(b, L, D_GROUPS, D_STATE)
    dt = jax.nn.softplus(dt_raw.astype(F32) + dt_bias.astype(F32))
    A = -jnp.exp(a_log.astype(F32))
    y, h = ssd(xh, dt, A, Bm, Cm, h0, q_len)
    y = y + d_skip.astype(F32)[:, None] * xh.astype(F32)
    y = y.reshape(b, L, D_INNER) * jax.nn.silu(z.astype(F32))
    yg = y.reshape(b, L, D_GROUPS, D_INNER // D_GROUPS)
    yg = yg * lax.rsqrt(jnp.mean(yg * yg, axis=-1, keepdims=True) + RMS_EPS)
    y = yg.reshape(b, L, D_INNER) * norm_g.astype(F32)
    return y.astype(z.dtype), new_hist, h


def cd_split(h, w_in):
    b, L, _ = h.shape
    proj = h @ w_in
    q, k, v, z, xbc, dt_raw = jnp.split(
        proj, [C_WIDTH, 2 * C_WIDTH, 3 * C_WIDTH, 3 * C_WIDTH + D_INNER,
               3 * C_WIDTH + D_INNER + D_XBC], axis=-1)
    shp = (b, L, C_HEADS, C_HEAD_DIM)
    return q.reshape(shp), k.reshape(shp), v.reshape(shp), z, xbc, dt_raw


def setup_inputs(seed: int = 0) -> dict:
    key = jax.random.key(seed)
    ks = jax.random.split(key, 32)

    def nrm(i, shape, scale):
        return scale * jax.random.normal(ks[i], shape, jnp.float32)

    c_len = min(C_BAND, PAST_LEN)
    u = jax.random.uniform(ks[21], (N_CD, D_HEADS), jnp.float32)
    dt0 = jnp.exp(u * (math.log(DT_MAX) - math.log(DT_MIN)) + math.log(DT_MIN))
    dt_bias = dt0 + jnp.log(-jnp.expm1(-dt0))
    a_log = jnp.log(jax.random.uniform(ks[22], (N_CD, D_HEADS), jnp.float32, 1.0, 16.0))
    return {
        'x_prompt': nrm(0, (BATCH, SEQ, D_MODEL), 1.0),
        'x_sample': nrm(1, (DEC_BATCH, DEC_SEQ, D_MODEL), 1.0),
        'cache_k_c': nrm(2, (N_CD, DEC_BATCH, c_len, C_HEADS, C_HEAD_DIM), 1.0),
        'cache_v_c': nrm(3, (N_CD, DEC_BATCH, c_len, C_HEADS, C_HEAD_DIM), 1.0),
        'state_conv_a': nrm(4, (N_AB, DEC_BATCH, A_CONV - 1, A_WIDTH), 1.0),
        'state_conv_d': nrm(5, (N_CD, DEC_BATCH, D_CONV - 1, D_XBC), 1.0),
        'state_ssm_d': nrm(6, (N_CD, DEC_BATCH, D_HEADS, D_HEAD_DIM, D_STATE), 0.1),
        'norm_mix': 1.0 + nrm(7, (DEPTH, D_MODEL), 0.1),
        'norm_ffn': 1.0 + nrm(8, (DEPTH, D_MODEL), 0.1),
        'norm_final': 1.0 + nrm(9, (D_MODEL,), 0.1),
        'w_in_ab': nrm(10, (N_AB, D_MODEL, AB_IN), D_MODEL ** -0.5),
        'conv_w_a': nrm(11, (N_AB, A_CONV, A_WIDTH), A_CONV ** -0.5),
        'ln_g_b': 1.0 + nrm(12, (N_AB, B_WIDTH), 0.1),
        'ln_b_b': nrm(13, (N_AB, B_WIDTH), 0.02),
        'w_s_b': nrm(14, (N_AB, B_GROUPS, B_CHUNK, B_CHUNK), B_CHUNK ** -0.5),
        'b_s_b': 1.0 + nrm(15, (N_AB, B_GROUPS, B_CHUNK), 0.1),
        'w_out_ab': nrm(16, (N_AB, MIX_WIDTH, D_MODEL), MIX_WIDTH ** -0.5),
        'w_in_cd': nrm(17, (N_CD, D_MODEL, CD_IN), D_MODEL ** -0.5),
        'rel_bias_c': nrm(18, (N_CD, C_HEADS, 2 * C_MAX_REL + 1), 0.5),
        'conv_w_d': nrm(19, (N_CD, D_CONV, D_XBC), D_CONV ** -0.5),
        'conv_b_d': nrm(20, (N_CD, D_XBC), 0.02),
        'dt_bias_d': dt_bias,
        'a_log_d': a_log,
        'd_skip_d': 1.0 + nrm(23, (N_CD, D_HEADS), 0.1),
        'norm_g_d': 1.0 + nrm(24, (N_CD, D_INNER), 0.1),
        'w_out_cd': nrm(25, (N_CD, MIX_WIDTH, D_MODEL), MIX_WIDTH ** -0.5),
        'w_gate': nrm(26, (DEPTH, D_MODEL, FFN_HIDDEN), D_MODEL ** -0.5),
        'w_up': nrm(27, (DEPTH, D_MODEL, FFN_HIDDEN), D_MODEL ** -0.5),
        'w_down': nrm(28, (DEPTH, FFN_HIDDEN, D_MODEL), FFN_HIDDEN ** -0.5),
    }


def reference(x_prompt, x_sample, cache_k_c, cache_v_c, state_conv_a, state_conv_d, state_ssm_d,
              norm_mix, norm_ffn, norm_final, w_in_ab, conv_w_a, ln_g_b, ln_b_b, w_s_b, b_s_b,
              w_out_ab, w_in_cd, rel_bias_c, conv_w_d, conv_b_d, dt_bias_d, a_log_d, d_skip_d,
              norm_g_d, w_out_cd, w_gate, w_up, w_down):
    hp, hs = x_prompt, x_sample
    bp, Lp, _ = hp.shape
    bs, Ls, _ = hs.shape
    conv_a_p, conv_a_s, v_b_s = [], [], []
    k_c_p, v_c_p, k_c_s, v_c_s = [], [], [], []
    conv_d_p, conv_d_s, ssm_d_p, ssm_d_s = [], [], [], []
    for layer in range(DEPTH):
        i = layer // 2
        hp_n = rmsnorm(hp, norm_mix[layer])
        hs_n = rmsnorm(hs, norm_mix[layer])
        if layer % 2 == 0:
            w = (w_in_ab[i], conv_w_a[i], ln_g_b[i], ln_b_b[i], w_s_b[i], b_s_b[i], w_out_ab[i])
            zero_hist = jnp.zeros((bp, A_CONV - 1, A_WIDTH), hp.dtype)
            mp, hist_p, _ = ab_layer(hp_n, zero_hist, *w)
            ms, hist_s, vrows_s = ab_layer(hs_n, state_conv_a[i], *w)
            conv_a_p.append(hist_p)
            conv_a_s.append(hist_s)
            v_b_s.append(vrows_s)
        else:
            dw = (conv_w_d[i], conv_b_d[i], dt_bias_d[i], a_log_d[i], d_skip_d[i], norm_g_d[i])
            q, k, v, z, xbc, dt_raw = cd_split(hp_n, w_in_cd[i])
            ap, kkeep, vkeep = mixer_c_prompt(q, k, v, rel_bias_c[i])
            dp, dhist_p, ssm_p = mixer_d(
                z, xbc, dt_raw, jnp.zeros((bp, D_CONV - 1, D_XBC), hp.dtype), *dw,
                jnp.zeros((bp, D_HEADS, D_HEAD_DIM, D_STATE), hp.dtype), CHUNK)
            mp = jnp.concatenate([ap, dp], axis=-1) @ w_out_cd[i]
            q, k, v, z, xbc, dt_raw = cd_split(hs_n, w_in_cd[i])
            a_s = mixer_c_sample(q, k, v, cache_k_c[i], cache_v_c[i], rel_bias_c[i])
            ds, dhist_s, ssm_s = mixer_d(z, xbc, dt_raw, state_conv_d[i], *dw, state_ssm_d[i], Ls)
            ms = jnp.concatenate([a_s, ds], axis=-1) @ w_out_cd[i]
            k_c_p.append(kkeep)
            v_c_p.append(vkeep)
            k_c_s.append(k)
            v_c_s.append(v)
            conv_d_p.append(dhist_p)
            conv_d_s.append(dhist_s)
            ssm_d_p.append(ssm_p)
            ssm_d_s.append(ssm_s)
        hp = hp + mp
        hs = hs + ms
        hp = hp + swiglu(rmsnorm(hp, norm_ffn[layer]), w_gate[layer], w_up[layer], w_down[layer])
        hs = hs + swiglu(rmsnorm(hs, norm_ffn[layer]), w_gate[layer], w_up[layer], w_down[layer])
    y_prompt = rmsnorm(hp, norm_final)
    y_sample = rmsnorm(hs, norm_final)
    new_conv_a_prompt = jnp.stack(conv_a_p)
    new_conv_a_sample = jnp.stack(conv_a_s)
    new_v_b_sample = jnp.stack(v_b_s)
    new_k_c_prompt = jnp.stack(k_c_p)
    new_v_c_prompt = jnp.stack(v_c_p)
    new_k_c_sample = jnp.stack(k_c_s)
    new_v_c_sample = jnp.stack(v_c_s)
    new_conv_d_prompt = jnp.stack(conv_d_p)
    new_conv_d_sample = jnp.stack(conv_d_s)
    new_ssm_d_prompt = jnp.stack(ssm_d_p)
    new_ssm_d_sample = jnp.stack(ssm_d_s)
    return (y_prompt, y_sample, new_conv_a_prompt, new_conv_a_sample, new_v_b_sample,
            new_k_c_prompt, new_v_c_prompt, new_k_c_sample, new_v_c_sample,
            new_conv_d_prompt, new_conv_d_sample, new_ssm_d_prompt, new_ssm_d_sample)
```

```python
import functools

import jax
import jax.numpy as jnp
from jax import lax
from jax.experimental import pallas as pl
from jax.experimental.pallas import tpu as pltpu

F32 = jnp.float32
BF16 = jnp.bfloat16

RMS_EPS = 1e-5
LN_EPS = 1e-5
CHUNK = 64
C_PREV_CHUNKS = 8
C_BAND = C_PREV_CHUNKS * CHUNK
C_MAX_REL = 128
HEAD_DIM = 64
HEADS_PER_SLAB = 4
SLAB = HEAD_DIM * HEADS_PER_SLAB
B_GROUPS = 4
B_GROUP_DIM = 128
B_CHUNK = 128
D_STATE = 128
NEG_BIG = -1e30

V7X_VMEM_LIMIT_BYTES = 56 * 1024 * 1024
PROMPT_TILE = 512
SSD_CHUNK = 128
ATTN_Q_BLOCK = 128

_NT = (((1,), (1,)), ((), ()))
_TN = (((0,), (0,)), ((), ()))


def _const_spec(shape):
    nd = len(shape)
    return pl.BlockSpec(shape, lambda *_: (0,) * nd, pipeline_mode=pl.Buffered(1))


def _rms(x, g):
    return x * lax.rsqrt(jnp.mean(x * x, axis=-1, keepdims=True) + RMS_EPS) * g


def _gelu(x):
    return 0.5 * x * (1.0 + lax.erf(x * (2.0 ** -0.5)))


def _silu(x):
    return x * jax.nn.sigmoid(x)


def _softplus(x):
    return jnp.maximum(x, 0.0) + jnp.log1p(jnp.exp(-jnp.abs(x)))


def _mm(a, b):
    return jnp.dot(a, b, preferred_element_type=F32)


def _ab_kernel(x_ref, hist_ref, ng_ref, win_ref, cw_ref, lng_ref, lnb_ref, ws_ref, bs_ref,
               wout_ref, *rest, tm, blk, width, emit_v):
    if emit_v:
        h_ref, nh_ref, v_ref, ubuf, mix = rest
    else:
        h_ref, nh_ref, ubuf, mix = rest
    a = width

    @pl.when(pl.program_id(1) == 0)
    def _():
        ubuf[0:8, :] = jnp.zeros((8, a), F32)
        ubuf[6:8, :] = hist_ref[0]

    x = x_ref[0]
    hn = _rms(x, ng_ref[...]).astype(BF16)
    proj = _mm(hn, win_ref[...])
    xa, gate_b, gate_c = proj[:, 0:a], proj[:, a:2 * a], proj[:, 2 * a:3 * a]
    u, v = proj[:, 3 * a:4 * a], proj[:, 4 * a:5 * a]

    ua = gate_c * xa
    ubuf[8:8 + tm, :] = ua
    cw = cw_ref[...]
    conv = cw[0:1] * ubuf[6:6 + tm, :] + cw[1:2] * ubuf[7:7 + tm, :] + cw[2:3] * ua
    mix[:, 0:a] = (gate_b * conv).astype(BF16)
    tail = ua[tm - 2:tm, :]
    ubuf[6:8, :] = tail
    nh_ref[0] = tail

    ug = _gelu(u)
    vg = _gelu(v)
    mu = jnp.mean(vg, axis=-1, keepdims=True)
    vc = vg - mu
    var = jnp.mean(vc * vc, axis=-1, keepdims=True)
    vn = vc * lax.rsqrt(var + LN_EPS) * lng_ref[...] + lnb_ref[...]
    if emit_v:
        v_ref[0] = vn
    vnb = vn.astype(BF16)
    for n in range(tm // blk):
        r0 = n * blk
        for g in range(B_GROUPS):
            c0 = g * B_GROUP_DIM
            f = _mm(ws_ref[g], vnb[r0:r0 + blk, c0:c0 + B_GROUP_DIM]) + bs_ref[g]
            mix[r0:r0 + blk, a + c0:a + c0 + B_GROUP_DIM] = (
                ug[r0:r0 + blk, c0:c0 + B_GROUP_DIM] * f).astype(BF16)

    h_ref[0] = _mm(mix[...], wout_ref[...]) + x


def _ab_layer(x, hist, norm_g, w_in, conv_w, ln_g, ln_b, w_s, b_s, w_out, *, tm, emit_v):
    bsz, seq, d = x.shape
    a = conv_w.shape[1]
    blk = min(seq, B_CHUNK)
    assert seq % tm == 0 and tm % blk == 0 and w_in.shape[1] == 5 * a
    tri = jnp.tril(jnp.ones((blk, blk), bool))
    ws = jnp.where(tri[None], w_s[:, :blk, :blk], 0).astype(BF16)
    bs = jnp.broadcast_to(b_s[:, :blk, None], (B_GROUPS, blk, B_GROUP_DIM)).astype(F32)

    out_shape = [jax.ShapeDtypeStruct((bsz, seq, d), F32),
                 jax.ShapeDtypeStruct((bsz, 2, a), F32)]
    out_specs = [pl.BlockSpec((1, tm, d), lambda b, t: (b, t, 0)),
                 pl.BlockSpec((1, 2, a), lambda b, t: (b, 0, 0))]
    if emit_v:
        out_shape.append(jax.ShapeDtypeStruct((bsz, seq, a), F32))
        out_specs.append(pl.BlockSpec((1, tm, a), lambda b, t: (b, t, 0)))

    return pl.pallas_call(
        functools.partial(_ab_kernel, tm=tm, blk=blk, width=a, emit_v=emit_v),
        grid=(bsz, seq // tm),
        in_specs=[
            pl.BlockSpec((1, tm, d), lambda b, t: (b, t, 0)),
            pl.BlockSpec((1, 2, a), lambda b, t: (b, 0, 0)),
            _const_spec((1, d)),
            _const_spec(w_in.shape),
            _const_spec(conv_w.shape),
            _const_spec((1, a)),
            _const_spec((1, a)),
            _const_spec(ws.shape),
            _const_spec(bs.shape),
            _const_spec(w_out.shape),
        ],
        out_specs=out_specs,
        out_shape=out_shape,
        scratch_shapes=[pltpu.VMEM((tm + 8, a), F32), pltpu.VMEM((tm, 2 * a), BF16)],
        compiler_params=pltpu.CompilerParams(
            dimension_semantics=("parallel", "arbitrary"),
            vmem_limit_bytes=V7X_VMEM_LIMIT_BYTES),
        name="ab_mixer",
    )(x, hist, norm_g.reshape(1, d), w_in.astype(BF16), conv_w, ln_g.reshape(1, a),
      ln_b.reshape(1, a), ws, bs, w_out.astype(BF16))


def _ffn_kernel(h_ref, ng_ref, wg_ref, wu_ref, wd_ref, *rest, final):
    if final:
        fg_ref, o_ref = rest
    else:
        (o_ref,) = rest
    h = h_ref[...]
    hn = _rms(h, ng_ref[...]).astype(BF16)
    gate = _mm(hn, wg_ref[...])
    up = _mm(hn, wu_ref[...])
    act = (_silu(gate) * up).astype(BF16)
    o = h + _mm(act, wd_ref[...])
    if final:
        o = _rms(o, fg_ref[...])
    o_ref[...] = o


def _ffn_layer(h, norm_g, w_gate, w_up, w_down, final_g, *, tm):
    shape = h.shape
    d = shape[-1]
    rows = h.size // d
    assert rows % tm == 0
    final = final_g is not None
    args = [h.reshape(rows, d), norm_g.reshape(1, d), w_gate.astype(BF16), w_up.astype(BF16),
            w_down.astype(BF16)]
    in_specs = [pl.BlockSpec((tm, d), lambda i: (i, 0)), _const_spec((1, d)),
                _const_spec(w_gate.shape), _const_spec(w_up.shape), _const_spec(w_down.shape)]
    if final:
        args.append(final_g.reshape(1, d))
        in_specs.append(_const_spec((1, d)))
    out = pl.pallas_call(
        functools.partial(_ffn_kernel, final=final),
        grid=(rows // tm,),
        in_specs=in_specs,
        out_specs=pl.BlockSpec((tm, d), lambda i: (i, 0)),
        out_shape=jax.ShapeDtypeStruct((rows, d), F32),
        compiler_params=pltpu.CompilerParams(
            dimension_semantics=("parallel",),
            vmem_limit_bytes=V7X_VMEM_LIMIT_BYTES),
        name="ffn_final" if final else "ffn",
    )(*args)
    return out.reshape(shape)


def _cd_kernel(*refs, tm, tq, q_len, n_t, has_hist):
    refs = list(refs)
    x_ref = refs.pop(0)
    if has_hist:
        kc_ref, vc_ref, ch_ref, st_ref = refs[:4]
        refs = refs[4:]
    (ng_ref, win_ref, wdt_ref, dtb_ref, alog_ref, bias_ref, cw_ref, cb_ref, dsk_ref, nrm_ref,
     wout_ref, h_ref, ko_ref, vo_ref, cho_ref, sto_ref, kbuf, vbuf, xbuf, st_t, mix) = refs
    t = pl.program_id(1)
    cw_dim = kbuf.shape[2]
    di = dsk_ref.shape[1]
    gn = D_STATE * (di // SLAB)
    xw = xbuf.shape[1]
    cur = lax.rem(t, 2)
    prev = 1 - cur

    @pl.when(t == 0)
    def _():
        xbuf[0:8, :] = jnp.zeros((8, xw), F32)
        if has_hist:
            kbuf[1] = kc_ref[0].astype(BF16)
            vbuf[1] = vc_ref[0].astype(BF16)
            xbuf[5:8, :] = ch_ref[0]
            st_t[...] = st_ref[0]
        else:
            kbuf[1] = jnp.zeros((C_BAND, cw_dim), BF16)
            vbuf[1] = jnp.zeros((C_BAND, cw_dim), BF16)
            st_t[...] = jnp.zeros(st_t.shape, F32)

    x = x_ref[0]
    hn = _rms(x, ng_ref[...]).astype(BF16)
    proj = _mm(hn, win_ref[...])
    c = cw_dim
    q = (proj[:, 0:c] * (HEAD_DIM ** -0.5)).astype(BF16)
    k = proj[:, c:2 * c]
    v = proj[:, 2 * c:3 * c]
    z = proj[:, 3 * c:3 * c + di]
    xbc = proj[:, 3 * c + di:3 * c + di + xw]
    ko_ref[0] = k
    vo_ref[0] = v
    kbuf[cur, 0:tm, :] = k.astype(BF16)
    vbuf[cur, 0:tm, :] = v.astype(BF16)

    slab_head = lax.broadcasted_iota(jnp.int32, (1, SLAB), 1) // HEAD_DIM
    zero_q = jnp.zeros((tq, SLAB), BF16)
    for qb in range(tm // tq):
        qs = qb * tq
        n_prev = C_BAND - qs
        n_cur = qs + tq
        for s_i in range(c // SLAB):
            c0 = s_i * SLAB
            qg = q[qs:qs + tq, c0:c0 + SLAB]
            kp = kbuf[prev, qs:C_BAND, c0:c0 + SLAB]
            vp = vbuf[prev, qs:C_BAND, c0:c0 + SLAB]
            kc = kbuf[cur, 0:n_cur, c0:c0 + SLAB]
            vc = vbuf[cur, 0:n_cur, c0:c0 + SLAB]
            acc = jnp.zeros((tq, SLAB), F32)
            for hh in range(HEADS_PER_SLAB):
                own = slab_head == hh
                qm = jnp.where(own, qg, zero_q)
                sp = lax.dot_general(qm, kp, _NT, preferred_element_type=F32)
                sc = lax.dot_general(qm, kc, _NT, preferred_element_type=F32)
                if not has_hist:
                    sp = jnp.where(t > 0, sp, NEG_BIG)
                s = jnp.concatenate([sp, sc], axis=1) + bias_ref[s_i * HEADS_PER_SLAB + hh]
                p = jnp.exp(s - jnp.max(s, axis=-1, keepdims=True))
                l = jnp.sum(p, axis=-1, keepdims=True)
                pb = p.astype(BF16)
                pv = (_mm(pb[:, 0:n_prev], jnp.where(own, vp, jnp.zeros_like(vp)))
                      + _mm(pb[:, n_prev:], jnp.where(own, vc, jnp.zeros_like(vc))))
                acc = acc + pv * (1.0 / l)
            mix[qs:qs + tq, c0:c0 + SLAB] = acc.astype(BF16)

    xbuf[8:8 + tm, :] = xbc
    tail = xbc[tm - 3:tm, :]
    cho_ref[0] = tail
    cw = cw_ref[...]
    conv = (cw[0:1] * xbuf[5:5 + tm, :] + cw[1:2] * xbuf[6:6 + tm, :]
            + cw[2:3] * xbuf[7:7 + tm, :] + cw[3:4] * xbc + cb_ref[...])
    xbuf[5:8, :] = tail
    xc = _silu(conv)
    xs, bm, cm = xc[:, 0:di], xc[:, di:di + gn], xc[:, di + gn:di + 2 * gn]
    dt = _softplus(_mm(hn, wdt_ref[...]) + dtb_ref[...])
    dta = dt * (-jnp.exp(alog_ref[...]))
    zg = _silu(z)
    dsk = dsk_ref[...]
    nrm = nrm_ref[...]

    row_i = lax.broadcasted_iota(jnp.int32, (q_len, q_len), 0)
    col_i = lax.broadcasted_iota(jnp.int32, (q_len, q_len), 1)
    tri = row_i >= col_i
    tri_f = tri.astype(F32)
    zero_x = jnp.zeros((q_len, SLAB), BF16)
    for ci in range(tm // q_len):
        r0 = ci * q_len
        cum = jnp.dot(tri_f, dta[r0:r0 + q_len, :], precision=lax.Precision.HIGHEST,
                      preferred_element_type=F32)
        cum_last = cum[q_len - 1:q_len, :]
        ecum = jnp.exp(cum)
        dend = jnp.exp(cum_last - cum)
        elast = jnp.exp(cum_last)
        if q_len % 128:
            pad = jnp.zeros((128 - q_len % 128, di), F32)
            cum_t = jnp.concatenate([cum, pad], axis=0).T
        else:
            cum_t = cum.T
        xsc = xs[r0:r0 + q_len, :]
        xdt = xsc * dt[r0:r0 + q_len, :]
        xdec = (xdt * dend).astype(BF16)
        xdt = xdt.astype(BF16)
        for g in range(di // SLAB):
            c0 = g * SLAB
            bg = bm[r0:r0 + q_len, g * D_STATE:(g + 1) * D_STATE].astype(BF16)
            cg = cm[r0:r0 + q_len, g * D_STATE:(g + 1) * D_STATE].astype(BF16)
            cb = lax.dot_general(cg, bg, _NT, preferred_element_type=F32)
            xg = xdt[:, c0:c0 + SLAB]
            y = jnp.zeros((q_len, SLAB), F32)
            for hh in range(HEADS_PER_SLAB):
                lane0 = c0 + hh * HEAD_DIM
                seg = cum[:, lane0:lane0 + 1] - cum_t[lane0:lane0 + 1, 0:q_len]
                decay = jnp.where(tri, jnp.exp(seg), 0.0)
                y = y + _mm((cb * decay).astype(BF16), jnp.where(slab_head == hh, xg, zero_x))
            st_old = st_t[:, c0:c0 + SLAB]
            y = y + _mm(cg, st_old.astype(BF16)) * ecum[:, c0:c0 + SLAB]
            st_t[:, c0:c0 + SLAB] = (st_old * elast[:, c0:c0 + SLAB]
                                     + lax.dot_general(bg, xdec[:, c0:c0 + SLAB], _TN,
                                                       preferred_element_type=F32))
            y = (y + dsk[:, c0:c0 + SLAB] * xsc[:, c0:c0 + SLAB]) * zg[r0:r0 + q_len, c0:c0 + SLAB]
            ms = jnp.mean(y * y, axis=-1, keepdims=True)
            mix[r0:r0 + q_len, c + c0:c + c0 + SLAB] = (
                y * lax.rsqrt(ms + RMS_EPS) * nrm[:, c0:c0 + SLAB]).astype(BF16)
    sto_ref[0] = st_t[...]

    h_ref[0] = _mm(mix[...], wout_ref[...]) + x


def _attn_bias(table, tq):
    nk = C_BAND + tq
    qi = jnp.arange(tq)[:, None]
    kj = jnp.arange(nk)[None, :]
    rel = jnp.clip(C_BAND + qi - kj, -C_MAX_REL, C_MAX_REL) + C_MAX_REL
    dchunk = kj // CHUNK - qi // CHUNK
    ok = (dchunk >= 0) & (dchunk <= C_PREV_CHUNKS)
    return jnp.where(ok[None], table[:, rel], NEG_BIG).astype(F32)


def _cd_layer(x, hist, norm_g, w_in, rel_bias, conv_w, conv_b, dt_bias, a_log, d_skip, norm_gd,
              w_out, *, tm):
    bsz, seq, d = x.shape
    heads = rel_bias.shape[0]
    c = heads * HEAD_DIM
    di = norm_gd.shape[0]
    d_heads = dt_bias.shape[0]
    xw = conv_w.shape[1]
    n_main = 3 * c + di + xw
    assert seq % tm == 0 and w_in.shape[1] == n_main + d_heads
    assert c % SLAB == 0 and di % SLAB == 0 and di // d_heads == HEAD_DIM
    tq = min(tm, ATTN_Q_BLOCK)
    q_len = min(tm, SSD_CHUNK)
    n_t = seq // tm
    keep = min(C_BAND, seq)
    assert tm == keep, "one tile must be exactly the K/V rows kept for the next call"
    has_hist = hist is not None

    rep = lambda p: jnp.repeat(p, HEAD_DIM).reshape(1, di)
    w_main = w_in[:, :n_main].astype(BF16)
    w_dt = jnp.repeat(w_in[:, n_main:], HEAD_DIM, axis=1).astype(BF16)
    bias = _attn_bias(rel_bias, tq)

    args = [x]
    in_specs = [pl.BlockSpec((1, tm, d), lambda b, t: (b, t, 0))]
    if has_hist:
        cache_k, cache_v, conv_rows, ssm = hist
        assert cache_k.shape[1] == C_BAND
        st_in = jnp.swapaxes(ssm.reshape(bsz, di, D_STATE), 1, 2)
        args += [cache_k.reshape(bsz, C_BAND, c), cache_v.reshape(bsz, C_BAND, c), conv_rows, st_in]
        in_specs += [pl.BlockSpec((1, C_BAND, c), lambda b, t: (b, 0, 0)),
                     pl.BlockSpec((1, C_BAND, c), lambda b, t: (b, 0, 0)),
                     pl.BlockSpec((1, 3, xw), lambda b, t: (b, 0, 0)),
                     pl.BlockSpec((1, D_STATE, di), lambda b, t: (b, 0, 0))]
    consts = [norm_g.reshape(1, d), w_main, w_dt, rep(dt_bias), rep(a_log), bias, conv_w,
              conv_b.reshape(1, xw), rep(d_skip), norm_gd.reshape(1, di), w_out.astype(BF16)]
    args += consts
    in_specs += [_const_spec(a.shape) for a in consts]

    out_shape = [jax.ShapeDtypeStruct((bsz, seq, d), F32),
                 jax.ShapeDtypeStruct((bsz, tm, c), F32),
                 jax.ShapeDtypeStruct((bsz, tm, c), F32),
                 jax.ShapeDtypeStruct((bsz, 3, xw), F32),
                 jax.ShapeDtypeStruct((bsz, D_STATE, di), F32)]
    out_specs = [pl.BlockSpec((1, tm, d), lambda b, t: (b, t, 0)),
                 pl.BlockSpec((1, tm, c), lambda b, t: (b, 0, 0)),
                 pl.BlockSpec((1, tm, c), lambda b, t: (b, 0, 0)),
                 pl.BlockSpec((1, 3, xw), lambda b, t: (b, 0, 0)),
                 pl.BlockSpec((1, D_STATE, di), lambda b, t: (b, 0, 0))]

    h, k_new, v_new, conv_new, st_new = pl.pallas_call(
        functools.partial(_cd_kernel, tm=tm, tq=tq, q_len=q_len, n_t=n_t, has_hist=has_hist),
        grid=(bsz, n_t),
        in_specs=in_specs,
        out_specs=out_specs,
        out_shape=out_shape,
        scratch_shapes=[pltpu.VMEM((2, C_BAND, c), BF16), pltpu.VMEM((2, C_BAND, c), BF16),
                        pltpu.VMEM((tm + 8, xw), F32), pltpu.VMEM((D_STATE, di), F32),
                        pltpu.VMEM((tm, c + di), BF16)],
        compiler_params=pltpu.CompilerParams(
            dimension_semantics=("parallel", "arbitrary"),
            vmem_limit_bytes=V7X_VMEM_LIMIT_BYTES),
        name="cd_mixer_hist" if has_hist else "cd_mixer",
    )(*args)
    k_new = k_new.reshape(bsz, tm, heads, HEAD_DIM)
    v_new = v_new.reshape(bsz, tm, heads, HEAD_DIM)
    st_new = jnp.swapaxes(st_new, 1, 2).reshape(bsz, d_heads, HEAD_DIM, D_STATE)
    return h, k_new, v_new, conv_new, st_new


def kernel(x_prompt, x_sample, cache_k_c, cache_v_c, state_conv_a, state_conv_d, state_ssm_d, norm_mix, norm_ffn, norm_final, w_in_ab, conv_w_a, ln_g_b, ln_b_b, w_s_b, b_s_b, w_out_ab, w_in_cd, rel_bias_c, conv_w_d, conv_b_d, dt_bias_d, a_log_d, d_skip_d, norm_g_d, w_out_cd, w_gate, w_up, w_down):
    bp, lp, _ = x_prompt.shape
    bs, ls, _ = x_sample.shape
    tp = min(PROMPT_TILE, lp)
    a_width = conv_w_a.shape[2]

    w_ab = (norm_mix[0], w_in_ab[0], conv_w_a[0], ln_g_b[0], ln_b_b[0], w_s_b[0], b_s_b[0],
            w_out_ab[0])
    hp, conv_a_p = _ab_layer(x_prompt, jnp.zeros((bp, 2, a_width), F32), *w_ab, tm=tp,
                             emit_v=False)
    hs, conv_a_s, v_b_s = _ab_layer(x_sample, state_conv_a[0], *w_ab, tm=ls, emit_v=True)
    w_f0 = (norm_ffn[0], w_gate[0], w_up[0], w_down[0])
    hp = _ffn_layer(hp, *w_f0, None, tm=tp)
    hs = _ffn_layer(hs, *w_f0, None, tm=min(bs * ls, PROMPT_TILE))

    w_cd = (norm_mix[1], w_in_cd[0], rel_bias_c[0], conv_w_d[0], conv_b_d[0], dt_bias_d[0],
            a_log_d[0], d_skip_d[0], norm_g_d[0], w_out_cd[0])
    hp, k_p, v_p, conv_d_p, ssm_p = _cd_layer(hp, None, *w_cd, tm=tp)
    hs, k_s, v_s, conv_d_s, ssm_s = _cd_layer(
        hs, (cache_k_c[0], cache_v_c[0], state_conv_d[0], state_ssm_d[0]), *w_cd, tm=ls)
    w_f1 = (norm_ffn[1], w_gate[1], w_up[1], w_down[1])
    y_p = _ffn_layer(hp, *w_f1, norm_final, tm=tp)
    y_s = _ffn_layer(hs, *w_f1, norm_final, tm=min(bs * ls, PROMPT_TILE))

    return (y_p, y_s, conv_a_p[None], conv_a_s[None], v_b_s[None], k_p[None], v_p[None],
            k_s[None], v_s[None], conv_d_p[None], conv_d_s[None], ssm_p[None], ssm_s[None])
```

```python
import functools

import jax
import jax.numpy as jnp
from jax import lax
from jax.experimental import pallas as pl
from jax.experimental.pallas import tpu as pltpu

F32 = jnp.float32
BF16 = jnp.bfloat16

RMS_EPS = 1e-5
LN_EPS = 1e-5
CHUNK = 64
C_PREV_CHUNKS = 8
C_BAND = C_PREV_CHUNKS * CHUNK
C_MAX_REL = 128
HEAD_DIM = 64
HEADS_PER_SLAB = 4
SLAB = HEAD_DIM * HEADS_PER_SLAB
B_GROUPS = 4
B_GROUP_DIM = 128
B_CHUNK = 128
D_STATE = 128
NEG_BIG = -1e30

V7X_VMEM_LIMIT_BYTES = 56 * 1024 * 1024
PROMPT_TILE = 512
SSD_CHUNK = 128
ATTN_Q_BLOCK = 128

_NT = (((1,), (1,)), ((), ()))
_TN = (((0,), (0,)), ((), ()))


def _const_spec(shape):
    nd = len(shape)
    return pl.BlockSpec(shape, lambda *_: (0,) * nd, pipeline_mode=pl.Buffered(1))


def _rms(x, g):
    return x * lax.rsqrt(jnp.mean(x * x, axis=-1, keepdims=True) + RMS_EPS) * g


def _gelu(x):
    return 0.5 * x * (1.0 + lax.erf(x * (2.0 ** -0.5)))


def _silu(x):
    return x * jax.nn.sigmoid(x)


def _softplus(x):
    return jnp.maximum(x, 0.0) + jnp.log1p(jnp.exp(-jnp.abs(x)))


def _mm(a, b):
    return jnp.dot(a, b, preferred_element_type=F32)


def _ab_kernel(x_ref, hist_ref, ng_ref, win_ref, cw_ref, lng_ref, lnb_ref, ws_ref, bs_ref,
               wout_ref, *rest, tm, blk, width, emit_v):
    if emit_v:
        h_ref, nh_ref, v_ref, ubuf, mix = rest
    else:
        h_ref, nh_ref, ubuf, mix = rest
    a = width

    @pl.when(pl.program_id(1) == 0)
    def _():
        ubuf[0:8, :] = jnp.zeros((8, a), F32)
        ubuf[6:8, :] = hist_ref[0]

    x = x_ref[0]
    hn = _rms(x, ng_ref[...]).astype(BF16)
    proj = _mm(hn, win_ref[...])
    xa, gate_b, gate_c = proj[:, 0:a], proj[:, a:2 * a], proj[:, 2 * a:3 * a]
    u, v = proj[:, 3 * a:4 * a], proj[:, 4 * a:5 * a]

    ua = gate_c * xa
    ubuf[8:8 + tm, :] = ua
    cw = cw_ref[...]
    conv = cw[0:1] * ubuf[6:6 + tm, :] + cw[1:2] * ubuf[7:7 + tm, :] + cw[2:3] * ua
    mix[:, 0:a] = (gate_b * conv).astype(BF16)
    tail = ua[tm - 2:tm, :]
    ubuf[6:8, :] = tail
    nh_ref[0] = tail

    ug = _gelu(u)
    vg = _gelu(v)
    mu = jnp.mean(vg, axis=-1, keepdims=True)
    vc = vg - mu
    var = jnp.mean(vc * vc, axis=-1, keepdims=True)
    vn = vc * lax.rsqrt(var + LN_EPS) * lng_ref[...] + lnb_ref[...]
    if emit_v:
        v_ref[0] = vn
    vnb = vn.astype(BF16)
    for n in range(tm // blk):
        r0 = n * blk
        for g in range(B_GROUPS):
            c0 = g * B_GROUP_DIM
            f = _mm(ws_ref[g], vnb[r0:r0 + blk, c0:c0 + B_GROUP_DIM]) + bs_ref[g]
            mix[r0:r0 + blk, a + c0:a + c0 + B_GROUP_DIM] = (
                ug[r0:r0 + blk, c0:c0 + B_GROUP_DIM] * f).astype(BF16)

    h_ref[0] = _mm(mix[...], wout_ref[...]) + x


def _ab_layer(x, hist, norm_g, w_in, conv_w, ln_g, ln_b, w_s, b_s, w_out, *, tm, emit_v):
    bsz, seq, d = x.shape
    a = conv_w.shape[1]
    blk = min(seq, B_CHUNK)
    assert seq % tm == 0 and tm % blk == 0 and w_in.shape[1] == 5 * a
    tri = jnp.tril(jnp.ones((blk, blk), bool))
    ws = jnp.where(tri[None], w_s[:, :blk, :blk], 0).astype(BF16)
    bs = jnp.broadcast_to(b_s[:, :blk, None], (B_GROUPS, blk, B_GROUP_DIM)).astype(F32)

    out_shape = [jax.ShapeDtypeStruct((bsz, seq, d), F32),
                 jax.ShapeDtypeStruct((bsz, 2, a), F32)]
    out_specs = [pl.BlockSpec((1, tm, d), lambda b, t: (b, t, 0)),
                 pl.BlockSpec((1, 2, a), lambda b, t: (b, 0, 0))]
    if emit_v:
        out_shape.append(jax.ShapeDtypeStruct((bsz, seq, a), F32))
        out_specs.append(pl.BlockSpec((1, tm, a), lambda b, t: (b, t, 0)))

    return pl.pallas_call(
        functools.partial(_ab_kernel, tm=tm, blk=blk, width=a, emit_v=emit_v),
        grid=(bsz, seq // tm),
        in_specs=[
            pl.BlockSpec((1, tm, d), lambda b, t: (b, t, 0)),
            pl.BlockSpec((1, 2, a), lambda b, t: (b, 0, 0)),
            _const_spec((1, d)),
            _const_spec(w_in.shape),
            _const_spec(conv_w.shape),
            _const_spec((1, a)),
            _const_spec((1, a)),
            _const_spec(ws.shape),
            _const_spec(bs.shape),
            _const_spec(w_out.shape),
        ],
        out_specs=out_specs,
        out_shape=out_shape,
        scratch_shapes=[pltpu.VMEM((tm + 8, a), F32), pltpu.VMEM((tm, 2 * a), BF16)],
        compiler_params=pltpu.CompilerParams(
            dimension_semantics=("parallel", "arbitrary"),
            vmem_limit_bytes=V7X_VMEM_LIMIT_BYTES),
        name="ab_mixer",
    )(x, hist, norm_g.reshape(1, d), w_in.astype(BF16), conv_w, ln_g.reshape(1, a),
      ln_b.reshape(1, a), ws, bs, w_out.astype(BF16))


def _ffn_kernel(h_ref, ng_ref, wg_ref, wu_ref, wd_ref, *rest, final):
    if final:
        fg_ref, o_ref = rest
    else:
        (o_ref,) = rest
    h = h_ref[...]
    hn = _rms(h, ng_ref[...]).astype(BF16)
    gate = _mm(hn, wg_ref[...])
    up = _mm(hn, wu_ref[...])
    act = (_silu(gate) * up).astype(BF16)
    o = h + _mm(act, wd_ref[...])
    if final:
        o = _rms(o, fg_ref[...])
    o_ref[...] = o


def _ffn_layer(h, norm_g, w_gate, w_up, w_down, final_g, *, tm):
    shape = h.shape
    d = shape[-1]
    rows = h.size // d
    assert rows % tm == 0
    final = final_g is not None
    args = [h.reshape(rows, d), norm_g.reshape(1, d), w_gate.astype(BF16), w_up.astype(BF16),
            w_down.astype(BF16)]
    in_specs = [pl.BlockSpec((tm, d), lambda i: (i, 0)), _const_spec((1, d)),
                _const_spec(w_gate.shape), _const_spec(w_up.shape), _const_spec(w_down.shape)]
    if final:
        args.append(final_g.reshape(1, d))
        in_specs.append(_const_spec((1, d)))
    out = pl.pallas_call(
        functools.partial(_ffn_kernel, final=final),
        grid=(rows // tm,),
        in_specs=in_specs,
        out_specs=pl.BlockSpec((tm, d), lambda i: (i, 0)),
        out_shape=jax.ShapeDtypeStruct((rows, d), F32),
        compiler_params=pltpu.CompilerParams(
            dimension_semantics=("parallel",),
            vmem_limit_bytes=V7X_VMEM_LIMIT_BYTES),
        name="ffn_final" if final else "ffn",
    )(*args)
    return out.reshape(shape)


def _cd_kernel(*refs, tm, tq, q_len, n_t, has_hist):
    refs = list(refs)
    x_ref = refs.pop(0)
    if has_hist:
        kc_ref, vc_ref, ch_ref, st_ref = refs[:4]
        refs = refs[4:]
    (ng_ref, win_ref, wdt_ref, dtb_ref, alog_ref, bias_ref, cw_ref, cb_ref, dsk_ref, nrm_ref,
     wout_ref, h_ref, ko_ref, vo_ref, cho_ref, sto_ref, kbuf, vbuf, xbuf, st_t, mix) = refs
    t = pl.program_id(1)
    cw_dim = kbuf.shape[2]
    di = dsk_ref.shape[1]
    gn = D_STATE * (di // SLAB)
    xw = xbuf.shape[1]
    cur = lax.rem(t, 2)
    prev = 1 - cur

    @pl.when(t == 0)
    def _():
        xbuf[0:8, :] = jnp.zeros((8, xw), F32)
        if has_hist:
            kbuf[1] = kc_ref[0].astype(BF16)
            vbuf[1] = vc_ref[0].astype(BF16)
            xbuf[5:8, :] = ch_ref[0]
            st_t[...] = st_ref[0]
        else:
            kbuf[1] = jnp.zeros((C_BAND, cw_dim), BF16)
            vbuf[1] = jnp.zeros((C_BAND, cw_dim), BF16)
            st_t[...] = jnp.zeros(st_t.shape, F32)

    x = x_ref[0]
    hn = _rms(x, ng_ref[...]).astype(BF16)
    proj = _mm(hn, win_ref[...])
    c = cw_dim
    q = (proj[:, 0:c] * (HEAD_DIM ** -0.5)).astype(BF16)
    k = proj[:, c:2 * c]
    v = proj[:, 2 * c:3 * c]
    z = proj[:, 3 * c:3 * c + di]
    xbc = proj[:, 3 * c + di:3 * c + di + xw]
    ko_ref[0] = k
    vo_ref[0] = v
    kbuf[cur, 0:tm, :] = k.astype(BF16)
    vbuf[cur, 0:tm, :] = v.astype(BF16)

    slab_head = lax.broadcasted_iota(jnp.int32, (1, SLAB), 1) // HEAD_DIM
    zero_q = jnp.zeros((tq, SLAB), BF16)
    for qb in range(tm // tq):
        qs = qb * tq
        n_prev = C_BAND - qs
        n_cur = qs + tq
        for s_i in range(c // SLAB):
            c0 = s_i * SLAB
            qg = q[qs:qs + tq, c0:c0 + SLAB]
            qst = jnp.concatenate([jnp.where(slab_head == hh, qg, zero_q)
                                   for hh in range(HEADS_PER_SLAB)], axis=0)
            kp = kbuf[prev, qs:C_BAND, c0:c0 + SLAB]
            kc = kbuf[cur, 0:n_cur, c0:c0 + SLAB]
            sp = lax.dot_general(qst, kp, _NT, preferred_element_type=F32)
            sc = lax.dot_general(qst, kc, _NT, preferred_element_type=F32)
            if not has_hist:
                sp = jnp.where(t > 0, sp, NEG_BIG)
            s = jnp.concatenate([sp, sc], axis=1) + bias_ref[s_i]
            p = jnp.exp(s - jnp.max(s, axis=-1, keepdims=True))
            l = jnp.sum(p, axis=-1, keepdims=True)
            pb = p.astype(BF16)
            pv = (_mm(pb[:, 0:n_prev], vbuf[prev, qs:C_BAND, c0:c0 + SLAB])
                  + _mm(pb[:, n_prev:], vbuf[cur, 0:n_cur, c0:c0 + SLAB])) * (1.0 / l)
            out = pv[0:tq]
            for hh in range(1, HEADS_PER_SLAB):
                out = jnp.where(slab_head == hh, pv[hh * tq:(hh + 1) * tq], out)
            mix[qs:qs + tq, c0:c0 + SLAB] = out.astype(BF16)

    xbuf[8:8 + tm, :] = xbc
    tail = xbc[tm - 3:tm, :]
    cho_ref[0] = tail
    cw = cw_ref[...]
    conv = (cw[0:1] * xbuf[5:5 + tm, :] + cw[1:2] * xbuf[6:6 + tm, :]
            + cw[2:3] * xbuf[7:7 + tm, :] + cw[3:4] * xbc + cb_ref[...])
    xbuf[5:8, :] = tail
    xc = _silu(conv)
    xs, bm, cm = xc[:, 0:di], xc[:, di:di + gn], xc[:, di + gn:di + 2 * gn]
    dt = _softplus(_mm(hn, wdt_ref[...]) + dtb_ref[...])
    dta = dt * (-jnp.exp(alog_ref[...]))
    zg = _silu(z)
    dsk = dsk_ref[...]
    nrm = nrm_ref[...]

    row_i = lax.broadcasted_iota(jnp.int32, (q_len, q_len), 0)
    col_i = lax.broadcasted_iota(jnp.int32, (q_len, q_len), 1)
    tri = row_i >= col_i
    tri_b = tri.astype(BF16)
    zero_x = jnp.zeros((q_len, SLAB), BF16)
    for ci in range(tm // q_len):
        r0 = ci * q_len
        dta_c = dta[r0:r0 + q_len, :]
        dta_hi = dta_c.astype(BF16)
        dta_lo = (dta_c - dta_hi.astype(F32)).astype(BF16)
        cum = _mm(tri_b, dta_hi) + _mm(tri_b, dta_lo)
        cum_last = cum[q_len - 1:q_len, :]
        ecum = jnp.exp(cum)
        dend = jnp.exp(cum_last - cum)
        elast = jnp.exp(cum_last)
        if q_len % 128:
            pad = jnp.zeros((128 - q_len % 128, di), F32)
            cum_t = jnp.concatenate([cum, pad], axis=0).T
        else:
            cum_t = cum.T
        xsc = xs[r0:r0 + q_len, :]
        xdt = xsc * dt[r0:r0 + q_len, :]
        xdec = (xdt * dend).astype(BF16)
        xdt = xdt.astype(BF16)
        for g in range(di // SLAB):
            c0 = g * SLAB
            bg = bm[r0:r0 + q_len, g * D_STATE:(g + 1) * D_STATE].astype(BF16)
            cg = cm[r0:r0 + q_len, g * D_STATE:(g + 1) * D_STATE].astype(BF16)
            cb = lax.dot_general(cg, bg, _NT, preferred_element_type=F32)
            xg = xdt[:, c0:c0 + SLAB]
            y = jnp.zeros((q_len, SLAB), F32)
            for hh in range(HEADS_PER_SLAB):
                lane0 = c0 + hh * HEAD_DIM
                seg = cum[:, lane0:lane0 + 1] - cum_t[lane0:lane0 + 1, 0:q_len]
                decay = jnp.where(tri, jnp.exp(seg), 0.0)
                y = y + _mm((cb * decay).astype(BF16), jnp.where(slab_head == hh, xg, zero_x))
            st_old = st_t[:, c0:c0 + SLAB]
            y = y + _mm(cg, st_old.astype(BF16)) * ecum[:, c0:c0 + SLAB]
            st_t[:, c0:c0 + SLAB] = (st_old * elast[:, c0:c0 + SLAB]
                                     + lax.dot_general(bg, xdec[:, c0:c0 + SLAB], _TN,
                                                       preferred_element_type=F32))
            y = (y + dsk[:, c0:c0 + SLAB] * xsc[:, c0:c0 + SLAB]) * zg[r0:r0 + q_len, c0:c0 + SLAB]
            ms = jnp.mean(y * y, axis=-1, keepdims=True)
            mix[r0:r0 + q_len, c + c0:c + c0 + SLAB] = (
                y * lax.rsqrt(ms + RMS_EPS) * nrm[:, c0:c0 + SLAB]).astype(BF16)
    sto_ref[0] = st_t[...]

    h_ref[0] = _mm(mix[...], wout_ref[...]) + x


def _attn_bias(table, tq):
    heads = table.shape[0]
    nk = C_BAND + tq
    span = nk + tq - 1
    n_flat = C_BAND - C_MAX_REL + tq
    low = C_MAX_REL + 1 - tq
    assert low >= 0 and n_flat + 2 * C_MAX_REL - low == span
    vec = jnp.concatenate([jnp.broadcast_to(table[:, 2 * C_MAX_REL:], (heads, n_flat)),
                           table[:, low:2 * C_MAX_REL][:, ::-1]], axis=1)
    flat = jnp.tile(vec, (1, tq + 1))[:, :tq * (span + 1)]
    bias = flat.reshape(heads, tq, span + 1)[:, ::-1, :nk]
    qi = jnp.arange(tq)[:, None]
    kj = jnp.arange(nk)[None, :]
    dchunk = kj // CHUNK - qi // CHUNK
    ok = (dchunk >= 0) & (dchunk <= C_PREV_CHUNKS)
    bias = jnp.where(ok[None], bias, NEG_BIG).astype(F32)
    return bias.reshape(heads // HEADS_PER_SLAB, HEADS_PER_SLAB * tq, nk)


def _cd_layer(x, hist, norm_g, w_in, rel_bias, conv_w, conv_b, dt_bias, a_log, d_skip, norm_gd,
              w_out, *, tm):
    bsz, seq, d = x.shape
    heads = rel_bias.shape[0]
    c = heads * HEAD_DIM
    di = norm_gd.shape[0]
    d_heads = dt_bias.shape[0]
    xw = conv_w.shape[1]
    n_main = 3 * c + di + xw
    assert seq % tm == 0 and w_in.shape[1] == n_main + d_heads
    assert c % SLAB == 0 and di % SLAB == 0 and di // d_heads == HEAD_DIM
    tq = min(tm, ATTN_Q_BLOCK)
    q_len = min(tm, SSD_CHUNK)
    n_t = seq // tm
    keep = min(C_BAND, seq)
    assert tm == keep, "one tile must be exactly the K/V rows kept for the next call"
    has_hist = hist is not None

    rep = lambda p: jnp.repeat(p, HEAD_DIM).reshape(1, di)
    w_main = w_in[:, :n_main].astype(BF16)
    w_dt = jnp.repeat(w_in[:, n_main:], HEAD_DIM, axis=1).astype(BF16)
    bias = _attn_bias(rel_bias, tq)

    args = [x]
    in_specs = [pl.BlockSpec((1, tm, d), lambda b, t: (b, t, 0))]
    if has_hist:
        cache_k, cache_v, conv_rows, ssm = hist
        assert cache_k.shape[1] == C_BAND
        st_in = jnp.swapaxes(ssm.reshape(bsz, di, D_STATE), 1, 2)
        args += [cache_k.reshape(bsz, C_BAND, c), cache_v.reshape(bsz, C_BAND, c), conv_rows, st_in]
        in_specs += [pl.BlockSpec((1, C_BAND, c), lambda b, t: (b, 0, 0)),
                     pl.BlockSpec((1, C_BAND, c), lambda b, t: (b, 0, 0)),
                     pl.BlockSpec((1, 3, xw), lambda b, t: (b, 0, 0)),
                     pl.BlockSpec((1, D_STATE, di), lambda b, t: (b, 0, 0))]
    consts = [norm_g.reshape(1, d), w_main, w_dt, rep(dt_bias), rep(a_log), bias, conv_w,
              conv_b.reshape(1, xw), rep(d_skip), norm_gd.reshape(1, di), w_out.astype(BF16)]
    args += consts
    in_specs += [_const_spec(a.shape) for a in consts]

    out_shape = [jax.ShapeDtypeStruct((bsz, seq, d), F32),
                 jax.ShapeDtypeStruct((bsz, tm, c), F32),
                 jax.ShapeDtypeStruct((bsz, tm, c), F32),
                 jax.ShapeDtypeStruct((bsz, 3, xw), F32),
                 jax.ShapeDtypeStruct((bsz, D_STATE, di), F32)]
    out_specs = [pl.BlockSpec((1, tm, d), lambda b, t: (b, t, 0)),
                 pl.BlockSpec((1, tm, c), lambda b, t: (b, 0, 0)),
                 pl.BlockSpec((1, tm, c), lambda b, t: (b, 0, 0)),
                 pl.BlockSpec((1, 3, xw), lambda b, t: (b, 0, 0)),
                 pl.BlockSpec((1, D_STATE, di), lambda b, t: (b, 0, 0))]

    h, k_new, v_new, conv_new, st_new = pl.pallas_call(
        functools.partial(_cd_kernel, tm=tm, tq=tq, q_len=q_len, n_t=n_t, has_hist=has_hist),
        grid=(bsz, n_t),
        in_specs=in_specs,
        out_specs=out_specs,
        out_shape=out_shape,
        scratch_shapes=[pltpu.VMEM((2, C_BAND, c), BF16), pltpu.VMEM((2, C_BAND, c), BF16),
                        pltpu.VMEM((tm + 8, xw), F32), pltpu.VMEM((D_STATE, di), F32),
                        pltpu.VMEM((tm, c + di), BF16)],
        compiler_params=pltpu.CompilerParams(
            dimension_semantics=("parallel", "arbitrary"),
            vmem_limit_bytes=V7X_VMEM_LIMIT_BYTES),
        name="cd_mixer_hist" if has_hist else "cd_mixer",
    )(*args)
    k_new = k_new.reshape(bsz, tm, heads, HEAD_DIM)
    v_new = v_new.reshape(bsz, tm, heads, HEAD_DIM)
    st_new = jnp.swapaxes(st_new, 1, 2).reshape(bsz, d_heads, HEAD_DIM, D_STATE)
    return h, k_new, v_new, conv_new, st_new


def kernel(x_prompt, x_sample, cache_k_c, cache_v_c, state_conv_a, state_conv_d, state_ssm_d, norm_mix, norm_ffn, norm_final, w_in_ab, conv_w_a, ln_g_b, ln_b_b, w_s_b, b_s_b, w_out_ab, w_in_cd, rel_bias_c, conv_w_d, conv_b_d, dt_bias_d, a_log_d, d_skip_d, norm_g_d, w_out_cd, w_gate, w_up, w_down):
    bp, lp, _ = x_prompt.shape
    bs, ls, _ = x_sample.shape
    tp = min(PROMPT_TILE, lp)
    a_width = conv_w_a.shape[2]

    w_ab = (norm_mix[0], w_in_ab[0], conv_w_a[0], ln_g_b[0], ln_b_b[0], w_s_b[0], b_s_b[0],
            w_out_ab[0])
    hp, conv_a_p = _ab_layer(x_prompt, jnp.zeros((bp, 2, a_width), F32), *w_ab, tm=tp,
                             emit_v=False)
    hs, conv_a_s, v_b_s = _ab_layer(x_sample, state_conv_a[0], *w_ab, tm=ls, emit_v=True)
    w_f0 = (norm_ffn[0], w_gate[0], w_up[0], w_down[0])
    hp = _ffn_layer(hp, *w_f0, None, tm=tp)
    hs = _ffn_layer(hs, *w_f0, None, tm=min(bs * ls, PROMPT_TILE))

    w_cd = (norm_mix[1], w_in_cd[0], rel_bias_c[0], conv_w_d[0], conv_b_d[0], dt_bias_d[0],
            a_log_d[0], d_skip_d[0], norm_g_d[0], w_out_cd[0])
    hp, k_p, v_p, conv_d_p, ssm_p = _cd_layer(hp, None, *w_cd, tm=tp)
    hs, k_s, v_s, conv_d_s, ssm_s = _cd_layer(
        hs, (cache_k_c[0], cache_v_c[0], state_conv_d[0], state_ssm_d[0]), *w_cd, tm=ls)
    w_f1 = (norm_ffn[1], w_gate[1], w_up[1], w_down[1])
    y_p = _ffn_layer(hp, *w_f1, norm_final, tm=tp)
    y_s = _ffn_layer(hs, *w_f1, norm_final, tm=min(bs * ls, PROMPT_TILE))

    return (y_p, y_s, conv_a_p[None], conv_a_s[None], v_b_s[None], k_p[None], v_p[None],
            k_s[None], v_s[None], conv_d_p[None], conv_d_s[None], ssm_p[None], ssm_s[None])
```

```python
import functools

import jax
import jax.numpy as jnp
from jax import lax
from jax.experimental import pallas as pl
from jax.experimental.pallas import tpu as pltpu

F32 = jnp.float32
BF16 = jnp.bfloat16

RMS_EPS = 1e-5
LN_EPS = 1e-5
CHUNK = 64
C_PREV_CHUNKS = 8
C_BAND = C_PREV_CHUNKS * CHUNK
C_MAX_REL = 128
HEAD_DIM = 64
HEADS_PER_SLAB = 4
SLAB = HEAD_DIM * HEADS_PER_SLAB
B_GROUPS = 4
B_GROUP_DIM = 128
B_CHUNK = 128
D_STATE = 128
NEG_BIG = -1e30

V7X_VMEM_LIMIT_BYTES = 56 * 1024 * 1024
PROMPT_TILE = 512
SSD_CHUNK = 128
ATTN_Q_BLOCK = 128

_NT = (((1,), (1,)), ((), ()))
_TN = (((0,), (0,)), ((), ()))


def _const_spec(shape):
    nd = len(shape)
    return pl.BlockSpec(shape, lambda *_: (0,) * nd, pipeline_mode=pl.Buffered(1))


def _rms(x, g):
    return x * lax.rsqrt(jnp.mean(x * x, axis=-1, keepdims=True) + RMS_EPS) * g


def _gelu(x):
    return 0.5 * x * (1.0 + lax.erf(x * (2.0 ** -0.5)))


def _silu(x):
    return x * jax.nn.sigmoid(x)


def _softplus(x):
    return jnp.maximum(x, 0.0) + jnp.log1p(jnp.exp(-jnp.abs(x)))


def _mm(a, b):
    return jnp.dot(a, b, preferred_element_type=F32)


def _ab_kernel(x_ref, hist_ref, ng_ref, win_ref, cw_ref, lng_ref, lnb_ref, ws_ref, bs_ref,
               wout_ref, *rest, tm, blk, width, emit_v):
    if emit_v:
        h_ref, nh_ref, v_ref, ubuf, mix = rest
    else:
        h_ref, nh_ref, ubuf, mix = rest
    a = width

    @pl.when(pl.program_id(1) == 0)
    def _():
        ubuf[0:8, :] = jnp.zeros((8, a), F32)
        ubuf[6:8, :] = hist_ref[0]

    x = x_ref[0]
    hn = _rms(x, ng_ref[...]).astype(BF16)

    def proj(i):
        return jnp.concatenate([_mm(hn, win_ref[:, i * a + j:i * a + j + SLAB])
                                for j in range(0, a, SLAB)], axis=1)

    v = proj(4)
    u = proj(3)
    vg = _gelu(v)
    mu = jnp.mean(vg, axis=-1, keepdims=True)
    vc = vg - mu
    var = jnp.mean(vc * vc, axis=-1, keepdims=True)
    vn = vc * lax.rsqrt(var + LN_EPS) * lng_ref[...] + lnb_ref[...]
    if emit_v:
        v_ref[0] = vn
    gate_c = proj(2)
    xa = proj(0)
    ug = _gelu(u)
    gate_b = proj(1)

    ua = gate_c * xa
    ubuf[8:8 + tm, :] = ua
    cw = cw_ref[...]
    conv = cw[0:1] * ubuf[6:6 + tm, :] + cw[1:2] * ubuf[7:7 + tm, :] + cw[2:3] * ua
    mix[:, 0:a] = (gate_b * conv).astype(BF16)
    tail = ua[tm - 2:tm, :]
    ubuf[6:8, :] = tail
    nh_ref[0] = tail

    vnb = vn.astype(BF16)
    for n in range(tm // blk):
        r0 = n * blk
        for g in range(B_GROUPS):
            c0 = g * B_GROUP_DIM
            f = _mm(ws_ref[g], vnb[r0:r0 + blk, c0:c0 + B_GROUP_DIM]) + bs_ref[g]
            mix[r0:r0 + blk, a + c0:a + c0 + B_GROUP_DIM] = (
                ug[r0:r0 + blk, c0:c0 + B_GROUP_DIM] * f).astype(BF16)

    h_ref[0] = _mm(mix[...], wout_ref[...]) + x


def _ab_layer(x, hist, norm_g, w_in, conv_w, ln_g, ln_b, w_s, b_s, w_out, *, tm, emit_v):
    bsz, seq, d = x.shape
    a = conv_w.shape[1]
    blk = min(seq, B_CHUNK)
    assert seq % tm == 0 and tm % blk == 0 and w_in.shape[1] == 5 * a and a % SLAB == 0
    tri = jnp.tril(jnp.ones((blk, blk), bool))
    ws = jnp.where(tri[None], w_s[:, :blk, :blk], 0).astype(BF16)
    bs = jnp.broadcast_to(b_s[:, :blk, None], (B_GROUPS, blk, B_GROUP_DIM)).astype(F32)

    out_shape = [jax.ShapeDtypeStruct((bsz, seq, d), F32),
                 jax.ShapeDtypeStruct((bsz, 2, a), F32)]
    out_specs = [pl.BlockSpec((1, tm, d), lambda b, t: (b, t, 0)),
                 pl.BlockSpec((1, 2, a), lambda b, t: (b, 0, 0))]
    if emit_v:
        out_shape.append(jax.ShapeDtypeStruct((bsz, seq, a), F32))
        out_specs.append(pl.BlockSpec((1, tm, a), lambda b, t: (b, t, 0)))

    return pl.pallas_call(
        functools.partial(_ab_kernel, tm=tm, blk=blk, width=a, emit_v=emit_v),
        grid=(bsz, seq // tm),
        in_specs=[
            pl.BlockSpec((1, tm, d), lambda b, t: (b, t, 0)),
            pl.BlockSpec((1, 2, a), lambda b, t: (b, 0, 0)),
            _const_spec((1, d)),
            _const_spec(w_in.shape),
            _const_spec(conv_w.shape),
            _const_spec((1, a)),
            _const_spec((1, a)),
            _const_spec(ws.shape),
            _const_spec(bs.shape),
            _const_spec(w_out.shape),
        ],
        out_specs=out_specs,
        out_shape=out_shape,
        scratch_shapes=[pltpu.VMEM((tm + 8, a), F32), pltpu.VMEM((tm, 2 * a), BF16)],
        compiler_params=pltpu.CompilerParams(
            dimension_semantics=("parallel", "arbitrary"),
            vmem_limit_bytes=V7X_VMEM_LIMIT_BYTES),
        name="ab_mixer",
    )(x, hist, norm_g.reshape(1, d), w_in.astype(BF16), conv_w, ln_g.reshape(1, a),
      ln_b.reshape(1, a), ws, bs, w_out.astype(BF16))


def _ffn_kernel(h_ref, ng_ref, wg_ref, wu_ref, wd_ref, *rest, final):
    if final:
        fg_ref, o_ref = rest
    else:
        (o_ref,) = rest
    h = h_ref[...]
    hn = _rms(h, ng_ref[...]).astype(BF16)
    gate = _mm(hn, wg_ref[...])
    up = _mm(hn, wu_ref[...])
    act = (_silu(gate) * up).astype(BF16)
    o = h + _mm(act, wd_ref[...])
    if final:
        o = _rms(o, fg_ref[...])
    o_ref[...] = o


def _ffn_layer(h, norm_g, w_gate, w_up, w_down, final_g, *, tm):
    shape = h.shape
    d = shape[-1]
    rows = h.size // d
    assert rows % tm == 0
    final = final_g is not None
    args = [h.reshape(rows, d), norm_g.reshape(1, d), w_gate.astype(BF16), w_up.astype(BF16),
            w_down.astype(BF16)]
    in_specs = [pl.BlockSpec((tm, d), lambda i: (i, 0)), _const_spec((1, d)),
                _const_spec(w_gate.shape), _const_spec(w_up.shape), _const_spec(w_down.shape)]
    if final:
        args.append(final_g.reshape(1, d))
        in_specs.append(_const_spec((1, d)))
    out = pl.pallas_call(
        functools.partial(_ffn_kernel, final=final),
        grid=(rows // tm,),
        in_specs=in_specs,
        out_specs=pl.BlockSpec((tm, d), lambda i: (i, 0)),
        out_shape=jax.ShapeDtypeStruct((rows, d), F32),
        compiler_params=pltpu.CompilerParams(
            dimension_semantics=("parallel",),
            vmem_limit_bytes=V7X_VMEM_LIMIT_BYTES),
        name="ffn_final" if final else "ffn",
    )(*args)
    return out.reshape(shape)


def _cd_kernel(*refs, tm, tq, q_len, n_t, has_hist):
    refs = list(refs)
    x_ref = refs.pop(0)
    if has_hist:
        kc_ref, vc_ref, ch_ref, st_ref = refs[:4]
        refs = refs[4:]
    (ng_ref, win_ref, wdt_ref, dtb_ref, alog_ref, bias_ref, cw_ref, cb_ref, dsk_ref, nrm_ref,
     wout_ref, h_ref, ko_ref, vo_ref, cho_ref, sto_ref, kbuf, vbuf, xbuf, st_t, mix) = refs
    t = pl.program_id(1)
    cw_dim = kbuf.shape[2]
    di = dsk_ref.shape[1]
    gn = D_STATE * (di // SLAB)
    xw = xbuf.shape[1]
    cur = lax.rem(t, 2)
    prev = 1 - cur

    @pl.when(t == 0)
    def _():
        xbuf[0:8, :] = jnp.zeros((8, xw), F32)
        if has_hist:
            kbuf[1] = kc_ref[0].astype(BF16)
            vbuf[1] = vc_ref[0].astype(BF16)
            xbuf[5:8, :] = ch_ref[0]
            st_t[...] = st_ref[0]
        else:
            kbuf[1] = jnp.zeros((C_BAND, cw_dim), BF16)
            vbuf[1] = jnp.zeros((C_BAND, cw_dim), BF16)
            st_t[...] = jnp.zeros(st_t.shape, F32)

    x = x_ref[0]
    hn = _rms(x, ng_ref[...]).astype(BF16)
    c = cw_dim
    z0, x0 = 3 * c, 3 * c + di

    def proj(col, w_ref=win_ref):
        return _mm(hn, w_ref[:, col:col + SLAB])

    for j in range(xw // SLAB):
        xbuf[8:8 + tm, j * SLAB:(j + 1) * SLAB] = proj(x0 + j * SLAB)
    tail = xbuf[tm + 5:tm + 8, :]
    cho_ref[0] = tail
    cw = cw_ref[...]
    cb_row = cb_ref[...]

    def conv_block(j):
        j0 = j * SLAB
        conv = (cw[0:1, j0:j0 + SLAB] * xbuf[5:5 + tm, j0:j0 + SLAB]
                + cw[1:2, j0:j0 + SLAB] * xbuf[6:6 + tm, j0:j0 + SLAB]
                + cw[2:3, j0:j0 + SLAB] * xbuf[7:7 + tm, j0:j0 + SLAB]
                + cw[3:4, j0:j0 + SLAB] * xbuf[8:8 + tm, j0:j0 + SLAB] + cb_row[:, j0:j0 + SLAB])
        return _silu(conv)

    n_slabs = di // SLAB
    assert xw // SLAB == 2 * n_slabs and c // SLAB == n_slabs
    dt_raw = [proj(g * SLAB, wdt_ref) for g in range(n_slabs)]
    xcs = [conv_block(j) for j in range(n_slabs)]
    z_raw = [proj(z0 + g * SLAB) for g in range(n_slabs)]
    xcs += [conv_block(n_slabs + j) for j in range(n_slabs)]
    xbuf[5:8, :] = tail
    q = jnp.concatenate([proj(g * SLAB) for g in range(n_slabs)], axis=1)
    dtb = dtb_ref[...]
    dt = jnp.concatenate([_softplus(dt_raw[g] + dtb[:, g * SLAB:(g + 1) * SLAB])
                          for g in range(n_slabs)], axis=1)
    k = jnp.concatenate([proj(c + g * SLAB) for g in range(n_slabs)], axis=1)
    zg = jnp.concatenate([_silu(zr) for zr in z_raw], axis=1)
    v = jnp.concatenate([proj(2 * c + g * SLAB) for g in range(n_slabs)], axis=1)
    q = (q * (HEAD_DIM ** -0.5)).astype(BF16)
    ko_ref[0] = k
    vo_ref[0] = v
    kbuf[cur, 0:tm, :] = k.astype(BF16)
    vbuf[cur, 0:tm, :] = v.astype(BF16)

    slab_head = lax.broadcasted_iota(jnp.int32, (1, SLAB), 1) // HEAD_DIM
    zero_q = jnp.zeros((tq, SLAB), BF16)
    for qb in range(tm // tq):
        qs = qb * tq
        n_prev = C_BAND - qs
        n_cur = qs + tq
        for s_i in range(c // SLAB):
            c0 = s_i * SLAB
            qg = q[qs:qs + tq, c0:c0 + SLAB]
            qst = jnp.concatenate([jnp.where(slab_head == hh, qg, zero_q)
                                   for hh in range(HEADS_PER_SLAB)], axis=0)
            kp = kbuf[prev, qs:C_BAND, c0:c0 + SLAB]
            kc = kbuf[cur, 0:n_cur, c0:c0 + SLAB]
            sp = lax.dot_general(qst, kp, _NT, preferred_element_type=F32)
            sc = lax.dot_general(qst, kc, _NT, preferred_element_type=F32)
            if not has_hist:
                sp = jnp.where(t > 0, sp, NEG_BIG)
            s = jnp.concatenate([sp, sc], axis=1) + bias_ref[s_i]
            p = jnp.exp(s - jnp.max(s, axis=-1, keepdims=True))
            l = jnp.sum(p, axis=-1, keepdims=True)
            pb = p.astype(BF16)
            pv = (_mm(pb[:, 0:n_prev], vbuf[prev, qs:C_BAND, c0:c0 + SLAB])
                  + _mm(pb[:, n_prev:], vbuf[cur, 0:n_cur, c0:c0 + SLAB])) * (1.0 / l)
            out = pv[0:tq]
            for hh in range(1, HEADS_PER_SLAB):
                out = jnp.where(slab_head == hh, pv[hh * tq:(hh + 1) * tq], out)
            mix[qs:qs + tq, c0:c0 + SLAB] = out.astype(BF16)

    xc = jnp.concatenate(xcs, axis=1)
    xs, bm, cm = xc[:, 0:di], xc[:, di:di + gn], xc[:, di + gn:di + 2 * gn]
    dta = dt * (-jnp.exp(alog_ref[...]))
    dsk = dsk_ref[...]
    nrm = nrm_ref[...]

    row_i = lax.broadcasted_iota(jnp.int32, (q_len, q_len), 0)
    col_i = lax.broadcasted_iota(jnp.int32, (q_len, q_len), 1)
    tri = row_i >= col_i
    tri_b = tri.astype(BF16)
    zero_x = jnp.zeros((q_len, SLAB), BF16)
    for ci in range(tm // q_len):
        r0 = ci * q_len
        dta_c = dta[r0:r0 + q_len, :]
        dta_hi = dta_c.astype(BF16)
        dta_lo = (dta_c - dta_hi.astype(F32)).astype(BF16)
        cum = _mm(tri_b, dta_hi) + _mm(tri_b, dta_lo)
        cum_last = cum[q_len - 1:q_len, :]
        ecum = jnp.exp(cum)
        dend = jnp.exp(cum_last - cum)
        elast = jnp.exp(cum_last)
        if q_len % 128:
            pad = jnp.zeros((128 - q_len % 128, di), F32)
            cum_t = jnp.concatenate([cum, pad], axis=0).T
        else:
            cum_t = cum.T
        xsc = xs[r0:r0 + q_len, :]
        xdt = xsc * dt[r0:r0 + q_len, :]
        xdec = (xdt * dend).astype(BF16)
        xdt = xdt.astype(BF16)
        for g in range(di // SLAB):
            c0 = g * SLAB
            bg = bm[r0:r0 + q_len, g * D_STATE:(g + 1) * D_STATE].astype(BF16)
            cg = cm[r0:r0 + q_len, g * D_STATE:(g + 1) * D_STATE].astype(BF16)
            cb = lax.dot_general(cg, bg, _NT, preferred_element_type=F32)
            xg = xdt[:, c0:c0 + SLAB]
            y = jnp.zeros((q_len, SLAB), F32)
            for hh in range(HEADS_PER_SLAB):
                lane0 = c0 + hh * HEAD_DIM
                seg = cum[:, lane0:lane0 + 1] - cum_t[lane0:lane0 + 1, 0:q_len]
                decay = jnp.where(tri, jnp.exp(seg), 0.0)
                y = y + _mm((cb * decay).astype(BF16), jnp.where(slab_head == hh, xg, zero_x))
            st_old = st_t[:, c0:c0 + SLAB]
            y = y + _mm(cg, st_old.astype(BF16)) * ecum[:, c0:c0 + SLAB]
            st_t[:, c0:c0 + SLAB] = (st_old * elast[:, c0:c0 + SLAB]
                                     + lax.dot_general(bg, xdec[:, c0:c0 + SLAB], _TN,
                                                       preferred_element_type=F32))
            y = (y + dsk[:, c0:c0 + SLAB] * xsc[:, c0:c0 + SLAB]) * zg[r0:r0 + q_len, c0:c0 + SLAB]
            ms = jnp.mean(y * y, axis=-1, keepdims=True)
            mix[r0:r0 + q_len, c + c0:c + c0 + SLAB] = (
                y * lax.rsqrt(ms + RMS_EPS) * nrm[:, c0:c0 + SLAB]).astype(BF16)
    sto_ref[0] = st_t[...]

    h_ref[0] = _mm(mix[...], wout_ref[...]) + x


def _attn_bias(table, tq):
    heads = table.shape[0]
    nk = C_BAND + tq
    span = nk + tq - 1
    n_flat = C_BAND - C_MAX_REL + tq
    low = C_MAX_REL + 1 - tq
    assert low >= 0 and n_flat + 2 * C_MAX_REL - low == span
    vec = jnp.concatenate([jnp.broadcast_to(table[:, 2 * C_MAX_REL:], (heads, n_flat)),
                           table[:, low:2 * C_MAX_REL][:, ::-1]], axis=1)
    flat = jnp.tile(vec, (1, tq + 1))[:, :tq * (span + 1)]
    bias = flat.reshape(heads, tq, span + 1)[:, ::-1, :nk]
    qi = jnp.arange(tq)[:, None]
    kj = jnp.arange(nk)[None, :]
    dchunk = kj // CHUNK - qi // CHUNK
    ok = (dchunk >= 0) & (dchunk <= C_PREV_CHUNKS)
    bias = jnp.where(ok[None], bias, NEG_BIG).astype(F32)
    return bias.reshape(heads // HEADS_PER_SLAB, HEADS_PER_SLAB * tq, nk)


def _cd_layer(x, hist, norm_g, w_in, rel_bias, conv_w, conv_b, dt_bias, a_log, d_skip, norm_gd,
              w_out, *, tm):
    bsz, seq, d = x.shape
    heads = rel_bias.shape[0]
    c = heads * HEAD_DIM
    di = norm_gd.shape[0]
    d_heads = dt_bias.shape[0]
    xw = conv_w.shape[1]
    n_main = 3 * c + di + xw
    assert seq % tm == 0 and w_in.shape[1] == n_main + d_heads
    assert c % SLAB == 0 and di % SLAB == 0 and di // d_heads == HEAD_DIM
    tq = min(tm, ATTN_Q_BLOCK)
    q_len = min(tm, SSD_CHUNK)
    n_t = seq // tm
    keep = min(C_BAND, seq)
    assert tm == keep, "one tile must be exactly the K/V rows kept for the next call"
    has_hist = hist is not None

    rep = lambda p: jnp.repeat(p, HEAD_DIM).reshape(1, di)
    w_main = w_in[:, :n_main].astype(BF16)
    w_dt = jnp.repeat(w_in[:, n_main:], HEAD_DIM, axis=1).astype(BF16)
    bias = _attn_bias(rel_bias, tq)

    args = [x]
    in_specs = [pl.BlockSpec((1, tm, d), lambda b, t: (b, t, 0))]
    if has_hist:
        cache_k, cache_v, conv_rows, ssm = hist
        assert cache_k.shape[1] == C_BAND
        st_in = jnp.swapaxes(ssm.reshape(bsz, di, D_STATE), 1, 2)
        args += [cache_k.reshape(bsz, C_BAND, c), cache_v.reshape(bsz, C_BAND, c), conv_rows, st_in]
        in_specs += [pl.BlockSpec((1, C_BAND, c), lambda b, t: (b, 0, 0)),
                     pl.BlockSpec((1, C_BAND, c), lambda b, t: (b, 0, 0)),
                     pl.BlockSpec((1, 3, xw), lambda b, t: (b, 0, 0)),
                     pl.BlockSpec((1, D_STATE, di), lambda b, t: (b, 0, 0))]
    consts = [norm_g.reshape(1, d), w_main, w_dt, rep(dt_bias), rep(a_log), bias, conv_w,
              conv_b.reshape(1, xw), rep(d_skip), norm_gd.reshape(1, di), w_out.astype(BF16)]
    args += consts
    in_specs += [_const_spec(a.shape) for a in consts]

    out_shape = [jax.ShapeDtypeStruct((bsz, seq, d), F32),
                 jax.ShapeDtypeStruct((bsz, tm, c), F32),
                 jax.ShapeDtypeStruct((bsz, tm, c), F32),
                 jax.ShapeDtypeStruct((bsz, 3, xw), F32),
                 jax.ShapeDtypeStruct((bsz, D_STATE, di), F32)]
    out_specs = [pl.BlockSpec((1, tm, d), lambda b, t: (b, t, 0)),
                 pl.BlockSpec((1, tm, c), lambda b, t: (b, 0, 0)),
                 pl.BlockSpec((1, tm, c), lambda b, t: (b, 0, 0)),
                 pl.BlockSpec((1, 3, xw), lambda b, t: (b, 0, 0)),
                 pl.BlockSpec((1, D_STATE, di), lambda b, t: (b, 0, 0))]

    h, k_new, v_new, conv_new, st_new = pl.pallas_call(
        functools.partial(_cd_kernel, tm=tm, tq=tq, q_len=q_len, n_t=n_t, has_hist=has_hist),
        grid=(bsz, n_t),
        in_specs=in_specs,
        out_specs=out_specs,
        out_shape=out_shape,
        scratch_shapes=[pltpu.VMEM((2, C_BAND, c), BF16), pltpu.VMEM((2, C_BAND, c), BF16),
                        pltpu.VMEM((tm + 8, xw), F32), pltpu.VMEM((D_STATE, di), F32),
                        pltpu.VMEM((tm, c + di), BF16)],
        compiler_params=pltpu.CompilerParams(
            dimension_semantics=("parallel", "arbitrary"),
            vmem_limit_bytes=V7X_VMEM_LIMIT_BYTES),
        name="cd_mixer_hist" if has_hist else "cd_mixer",
    )(*args)
    k_new = k_new.reshape(bsz, tm, heads, HEAD_DIM)
    v_new = v_new.reshape(bsz, tm, heads, HEAD_DIM)
    st_new = jnp.swapaxes(st_new, 1, 2).reshape(bsz, d_heads, HEAD_DIM, D_STATE)
    return h, k_new, v_new, conv_new, st_new


def kernel(x_prompt, x_sample, cache_k_c, cache_v_c, state_conv_a, state_conv_d, state_ssm_d, norm_mix, norm_ffn, norm_final, w_in_ab, conv_w_a, ln_g_b, ln_b_b, w_s_b, b_s_b, w_out_ab, w_in_cd, rel_bias_c, conv_w_d, conv_b_d, dt_bias_d, a_log_d, d_skip_d, norm_g_d, w_out_cd, w_gate, w_up, w_down):
    bp, lp, _ = x_prompt.shape
    bs, ls, _ = x_sample.shape
    tp = min(PROMPT_TILE, lp)
    a_width = conv_w_a.shape[2]

    w_ab = (norm_mix[0], w_in_ab[0], conv_w_a[0], ln_g_b[0], ln_b_b[0], w_s_b[0], b_s_b[0],
            w_out_ab[0])
    hp, conv_a_p = _ab_layer(x_prompt, jnp.zeros((bp, 2, a_width), F32), *w_ab, tm=tp,
                             emit_v=False)
    hs, conv_a_s, v_b_s = _ab_layer(x_sample, state_conv_a[0], *w_ab, tm=ls, emit_v=True)
    w_f0 = (norm_ffn[0], w_gate[0], w_up[0], w_down[0])
    hp = _ffn_layer(hp, *w_f0, None, tm=tp)
    hs = _ffn_layer(hs, *w_f0, None, tm=min(bs * ls, PROMPT_TILE))

    w_cd = (norm_mix[1], w_in_cd[0], rel_bias_c[0], conv_w_d[0], conv_b_d[0], dt_bias_d[0],
            a_log_d[0], d_skip_d[0], norm_g_d[0], w_out_cd[0])
    hp, k_p, v_p, conv_d_p, ssm_p = _cd_layer(hp, None, *w_cd, tm=tp)
    hs, k_s, v_s, conv_d_s, ssm_s = _cd_layer(
        hs, (cache_k_c[0], cache_v_c[0], state_conv_d[0], state_ssm_d[0]), *w_cd, tm=ls)
    w_f1 = (norm_ffn[1], w_gate[1], w_up[1], w_down[1])
    y_p = _ffn_layer(hp, *w_f1, norm_final, tm=tp)
    y_s = _ffn_layer(hs, *w_f1, norm_final, tm=min(bs * ls, PROMPT_TILE))

    return (y_p, y_s, conv_a_p[None], conv_a_s[None], v_b_s[None], k_p[None], v_p[None],
            k_s[None], v_s[None], conv_d_p[None], conv_d_s[None], ssm_p[None], ssm_s[None])
```

```python
import functools

import jax
import jax.numpy as jnp
from jax import lax
from jax.experimental import pallas as pl
from jax.experimental.pallas import tpu as pltpu

F32 = jnp.float32
BF16 = jnp.bfloat16

RMS_EPS = 1e-5
LN_EPS = 1e-5
CHUNK = 64
C_PREV_CHUNKS = 8
C_BAND = C_PREV_CHUNKS * CHUNK
C_MAX_REL = 128
HEAD_DIM = 64
HEADS_PER_SLAB = 4
SLAB = HEAD_DIM * HEADS_PER_SLAB
B_GROUPS = 4
B_GROUP_DIM = 128
B_CHUNK = 128
D_STATE = 128
NEG_BIG = -1e30

V7X_VMEM_LIMIT_BYTES = 56 * 1024 * 1024
PROMPT_TILE = 512
SSD_CHUNK = 128
ATTN_Q_BLOCK = 128
SOFTMAX_ROWS = 16
VEC_ROWS = 32
LOG2E = 1.4426950408889634

_NT = (((1,), (1,)), ((), ()))
_TN = (((0,), (0,)), ((), ()))


def _const_spec(shape):
    nd = len(shape)
    return pl.BlockSpec(shape, lambda *_: (0,) * nd, pipeline_mode=pl.Buffered(1))


def _rms(x, g):
    return x * lax.rsqrt(jnp.mean(x * x, axis=-1, keepdims=True) + RMS_EPS) * g


def _gelu(x):
    return 0.5 * x * (1.0 + lax.erf(x * (2.0 ** -0.5)))


def _silu(x):
    hx = 0.5 * x
    return hx + hx * jnp.tanh(hx)


def _softplus(x):
    return jnp.maximum(x, 0.0) + jnp.log1p(jnp.exp(-jnp.abs(x)))


def _mm(a, b):
    return jnp.dot(a, b, preferred_element_type=F32)


def _rowwise(fn, *arrays, rows):
    n = arrays[0].shape[0]
    if n <= rows:
        return fn(*arrays)
    return jnp.concatenate([fn(*(a[r:r + rows] for a in arrays)) for r in range(0, n, rows)],
                           axis=0)


def _ab_kernel(x_ref, hist_ref, ng_ref, win_ref, cw_ref, lng_ref, lnb_ref, ws_ref, bs_ref,
               wout_ref, *rest, tm, blk, width, emit_v):
    if emit_v:
        h_ref, nh_ref, v_ref, ubuf, mix = rest
    else:
        h_ref, nh_ref, ubuf, mix = rest
    a = width

    @pl.when(pl.program_id(1) == 0)
    def _():
        ubuf[0:8, :] = jnp.zeros((8, a), F32)
        ubuf[6:8, :] = hist_ref[0]

    x = x_ref[0]
    hn = _rms(x, ng_ref[...]).astype(BF16)

    def proj(i):
        return jnp.concatenate([_mm(hn, win_ref[:, i * a + j:i * a + j + SLAB])
                                for j in range(0, a, SLAB)], axis=1)

    v = proj(4)
    u = proj(3)
    vg = _gelu(v)
    mu = jnp.mean(vg, axis=-1, keepdims=True)
    vc = vg - mu
    var = jnp.mean(vc * vc, axis=-1, keepdims=True)
    vn = vc * lax.rsqrt(var + LN_EPS) * lng_ref[...] + lnb_ref[...]
    if emit_v:
        v_ref[0] = vn
    gate_c = proj(2)
    xa = proj(0)
    ug = _gelu(u)
    gate_b = proj(1)

    ua = gate_c * xa
    ubuf[8:8 + tm, :] = ua
    cw = cw_ref[...]
    conv = cw[0:1] * ubuf[6:6 + tm, :] + cw[1:2] * ubuf[7:7 + tm, :] + cw[2:3] * ua
    mix[:, 0:a] = (gate_b * conv).astype(BF16)
    tail = ua[tm - 2:tm, :]
    ubuf[6:8, :] = tail
    nh_ref[0] = tail

    vnb = vn.astype(BF16)
    for n in range(tm // blk):
        r0 = n * blk
        for g in range(B_GROUPS):
            c0 = g * B_GROUP_DIM
            f = _mm(ws_ref[g], vnb[r0:r0 + blk, c0:c0 + B_GROUP_DIM]) + bs_ref[g]
            mix[r0:r0 + blk, a + c0:a + c0 + B_GROUP_DIM] = (
                ug[r0:r0 + blk, c0:c0 + B_GROUP_DIM] * f).astype(BF16)

    h_ref[0] = _mm(mix[...], wout_ref[...]) + x


def _ab_layer(x, hist, norm_g, w_in, conv_w, ln_g, ln_b, w_s, b_s, w_out, *, tm, emit_v):
    bsz, seq, d = x.shape
    a = conv_w.shape[1]
    blk = min(seq, B_CHUNK)
    assert seq % tm == 0 and tm % blk == 0 and w_in.shape[1] == 5 * a and a % SLAB == 0
    tri = jnp.tril(jnp.ones((blk, blk), bool))
    ws = jnp.where(tri[None], w_s[:, :blk, :blk], 0).astype(BF16)
    bs = jnp.broadcast_to(b_s[:, :blk, None], (B_GROUPS, blk, B_GROUP_DIM)).astype(F32)

    out_shape = [jax.ShapeDtypeStruct((bsz, seq, d), F32),
                 jax.ShapeDtypeStruct((bsz, 2, a), F32)]
    out_specs = [pl.BlockSpec((1, tm, d), lambda b, t: (b, t, 0)),
                 pl.BlockSpec((1, 2, a), lambda b, t: (b, 0, 0))]
    if emit_v:
        out_shape.append(jax.ShapeDtypeStruct((bsz, seq, a), F32))
        out_specs.append(pl.BlockSpec((1, tm, a), lambda b, t: (b, t, 0)))

    return pl.pallas_call(
        functools.partial(_ab_kernel, tm=tm, blk=blk, width=a, emit_v=emit_v),
        grid=(bsz, seq // tm),
        in_specs=[
            pl.BlockSpec((1, tm, d), lambda b, t: (b, t, 0)),
            pl.BlockSpec((1, 2, a), lambda b, t: (b, 0, 0)),
            _const_spec((1, d)),
            _const_spec(w_in.shape),
            _const_spec(conv_w.shape),
            _const_spec((1, a)),
            _const_spec((1, a)),
            _const_spec(ws.shape),
            _const_spec(bs.shape),
            _const_spec(w_out.shape),
        ],
        out_specs=out_specs,
        out_shape=out_shape,
        scratch_shapes=[pltpu.VMEM((tm + 8, a), F32), pltpu.VMEM((tm, 2 * a), BF16)],
        compiler_params=pltpu.CompilerParams(
            dimension_semantics=("parallel", "arbitrary"),
            vmem_limit_bytes=V7X_VMEM_LIMIT_BYTES),
        name="ab_mixer",
    )(x, hist, norm_g.reshape(1, d), w_in.astype(BF16), conv_w, ln_g.reshape(1, a),
      ln_b.reshape(1, a), ws, bs, w_out.astype(BF16))


def _ffn_kernel(h_ref, ng_ref, wg_ref, wu_ref, wd_ref, *rest, final):
    if final:
        fg_ref, o_ref = rest
    else:
        (o_ref,) = rest
    h = h_ref[...]
    hn = _rms(h, ng_ref[...]).astype(BF16)
    gate = _mm(hn, wg_ref[...])
    up = _mm(hn, wu_ref[...])
    act = (_silu(gate) * up).astype(BF16)
    o = h + _mm(act, wd_ref[...])
    if final:
        o = _rms(o, fg_ref[...])
    o_ref[...] = o


def _ffn_layer(h, norm_g, w_gate, w_up, w_down, final_g, *, tm):
    shape = h.shape
    d = shape[-1]
    rows = h.size // d
    assert rows % tm == 0
    final = final_g is not None
    args = [h.reshape(rows, d), norm_g.reshape(1, d), w_gate.astype(BF16), w_up.astype(BF16),
            w_down.astype(BF16)]
    in_specs = [pl.BlockSpec((tm, d), lambda i: (i, 0)), _const_spec((1, d)),
                _const_spec(w_gate.shape), _const_spec(w_up.shape), _const_spec(w_down.shape)]
    if final:
        args.append(final_g.reshape(1, d))
        in_specs.append(_const_spec((1, d)))
    out = pl.pallas_call(
        functools.partial(_ffn_kernel, final=final),
        grid=(rows // tm,),
        in_specs=in_specs,
        out_specs=pl.BlockSpec((tm, d), lambda i: (i, 0)),
        out_shape=jax.ShapeDtypeStruct((rows, d), F32),
        compiler_params=pltpu.CompilerParams(
            dimension_semantics=("parallel",),
            vmem_limit_bytes=V7X_VMEM_LIMIT_BYTES),
        name="ffn_final" if final else "ffn",
    )(*args)
    return out.reshape(shape)


def _cd_kernel(*refs, tm, tq, q_len, n_t, has_hist):
    refs = list(refs)
    x_ref = refs.pop(0)
    if has_hist:
        kc_ref, vc_ref, ch_ref, st_ref = refs[:4]
        refs = refs[4:]
    (ng_ref, win_ref, wdt_ref, dtb_ref, alog_ref, bias_ref, cw_ref, cb_ref, dsk_ref, nrm_ref,
     wout_ref, h_ref, ko_ref, vo_ref, cho_ref, sto_ref, kbuf, vbuf, xbuf, st_t, mix, pbuf) = refs
    t = pl.program_id(1)
    cw_dim = kbuf.shape[2]
    di = dsk_ref.shape[1]
    gn = D_STATE * (di // SLAB)
    xw = xbuf.shape[1]
    cur = lax.rem(t, 2)
    prev = 1 - cur

    @pl.when(t == 0)
    def _():
        xbuf[0:8, :] = jnp.zeros((8, xw), F32)
        if has_hist:
            kbuf[1] = kc_ref[0].astype(BF16)
            vbuf[1] = vc_ref[0].astype(BF16)
            xbuf[5:8, :] = ch_ref[0]
            st_t[...] = st_ref[0]
        else:
            kbuf[1] = jnp.zeros((C_BAND, cw_dim), BF16)
            vbuf[1] = jnp.zeros((C_BAND, cw_dim), BF16)
            st_t[...] = jnp.zeros(st_t.shape, F32)

    x = x_ref[0]
    hn = _rms(x, ng_ref[...]).astype(BF16)
    c = cw_dim
    z0, x0 = 3 * c, 3 * c + di

    def proj(col, w_ref=win_ref):
        return _mm(hn, w_ref[:, col:col + SLAB])

    for j in range(xw // SLAB):
        xbuf[8:8 + tm, j * SLAB:(j + 1) * SLAB] = proj(x0 + j * SLAB)
    tail = xbuf[tm + 5:tm + 8, :]
    cho_ref[0] = tail
    cw = cw_ref[...]
    cb_row = cb_ref[...]

    def conv_block(j):
        j0 = j * SLAB
        parts = []
        for r in range(0, tm, VEC_ROWS):
            n = min(VEC_ROWS, tm - r)
            conv = (cw[0:1, j0:j0 + SLAB] * xbuf[5 + r:5 + r + n, j0:j0 + SLAB]
                    + cw[1:2, j0:j0 + SLAB] * xbuf[6 + r:6 + r + n, j0:j0 + SLAB]
                    + cw[2:3, j0:j0 + SLAB] * xbuf[7 + r:7 + r + n, j0:j0 + SLAB]
                    + cw[3:4, j0:j0 + SLAB] * xbuf[8 + r:8 + r + n, j0:j0 + SLAB]
                    + cb_row[:, j0:j0 + SLAB])
            parts.append(_silu(conv))
        return jnp.concatenate(parts, axis=0)

    n_slabs = di // SLAB
    assert xw // SLAB == 2 * n_slabs and c // SLAB == n_slabs
    dt_raw = [proj(g * SLAB, wdt_ref) for g in range(n_slabs)]
    xcs = [conv_block(j) for j in range(n_slabs)]
    z_raw = [proj(z0 + g * SLAB) for g in range(n_slabs)]
    xcs += [conv_block(n_slabs + j) for j in range(n_slabs)]
    xbuf[5:8, :] = tail
    q = jnp.concatenate([proj(g * SLAB) for g in range(n_slabs)], axis=1)
    dtb = dtb_ref[...]
    dt = jnp.concatenate(
        [_rowwise(lambda r, g=g: _softplus(r + dtb[:, g * SLAB:(g + 1) * SLAB]), dt_raw[g],
                  rows=VEC_ROWS) for g in range(n_slabs)], axis=1)
    k = jnp.concatenate([proj(c + g * SLAB) for g in range(n_slabs)], axis=1)
    zg = jnp.concatenate([_rowwise(_silu, zr, rows=VEC_ROWS) for zr in z_raw], axis=1)
    v = jnp.concatenate([proj(2 * c + g * SLAB) for g in range(n_slabs)], axis=1)
    q = (q * (HEAD_DIM ** -0.5 * LOG2E)).astype(BF16)
    ko_ref[0] = k
    vo_ref[0] = v
    kbuf[cur, 0:tm, :] = k.astype(BF16)
    vbuf[cur, 0:tm, :] = v.astype(BF16)

    slab_head = lax.broadcasted_iota(jnp.int32, (1, SLAB), 1) // HEAD_DIM
    zero_q = jnp.zeros((tq, SLAB), BF16)
    for qb in range(tm // tq):
        qs = qb * tq
        n_prev = C_BAND - qs
        n_cur = qs + tq
        for s_i in range(c // SLAB):
            c0 = s_i * SLAB
            qg = q[qs:qs + tq, c0:c0 + SLAB]
            qst = jnp.concatenate([jnp.where(slab_head == hh, qg, zero_q)
                                   for hh in range(HEADS_PER_SLAB)], axis=0)
            kp = kbuf[prev, qs:C_BAND, c0:c0 + SLAB]
            kc = kbuf[cur, 0:n_cur, c0:c0 + SLAB]
            sp = lax.dot_general(qst, kp, _NT, preferred_element_type=F32)
            sc = lax.dot_general(qst, kc, _NT, preferred_element_type=F32)
            ls = []
            for r in range(0, HEADS_PER_SLAB * tq, SOFTMAX_ROWS):
                spr = sp[r:r + SOFTMAX_ROWS]
                if not has_hist:
                    spr = jnp.where(t > 0, spr, NEG_BIG)
                sb = (jnp.concatenate([spr, sc[r:r + SOFTMAX_ROWS]], axis=1)
                      + bias_ref[s_i, r:r + SOFTMAX_ROWS, :])
                pr = jnp.exp2(sb - jnp.max(sb, axis=-1, keepdims=True))
                ls.append(jnp.sum(pr, axis=-1, keepdims=True))
                pbuf[r:r + SOFTMAX_ROWS, 0:n_prev + n_cur] = pr.astype(BF16)
            l = jnp.concatenate(ls, axis=0)
            pv = (_mm(pbuf[:, 0:n_prev], vbuf[prev, qs:C_BAND, c0:c0 + SLAB])
                  + _mm(pbuf[:, n_prev:n_prev + n_cur], vbuf[cur, 0:n_cur, c0:c0 + SLAB])) * (1.0 / l)
            out = pv[0:tq]
            for hh in range(1, HEADS_PER_SLAB):
                out = jnp.where(slab_head == hh, pv[hh * tq:(hh + 1) * tq], out)
            mix[qs:qs + tq, c0:c0 + SLAB] = out.astype(BF16)

    xc = jnp.concatenate(xcs, axis=1)
    xs, bm, cm = xc[:, 0:di], xc[:, di:di + gn], xc[:, di + gn:di + 2 * gn]
    dta = dt * (-jnp.exp(alog_ref[...]))
    dsk = dsk_ref[...]
    nrm = nrm_ref[...]

    row_i = lax.broadcasted_iota(jnp.int32, (q_len, q_len), 0)
    col_i = lax.broadcasted_iota(jnp.int32, (q_len, q_len), 1)
    tri = row_i >= col_i
    tri_b = tri.astype(BF16)
    zero_x = jnp.zeros((q_len, SLAB), BF16)
    for ci in range(tm // q_len):
        r0 = ci * q_len
        dta_c = dta[r0:r0 + q_len, :]
        dta_hi = dta_c.astype(BF16)
        dta_lo = (dta_c - dta_hi.astype(F32)).astype(BF16)
        cum = _mm(tri_b, dta_hi) + _mm(tri_b, dta_lo)
        cum_last = cum[q_len - 1:q_len, :]
        ecum = _rowwise(jnp.exp, cum, rows=VEC_ROWS)
        elast = jnp.exp(cum_last)
        if q_len % 128:
            pad = jnp.zeros((128 - q_len % 128, di), F32)
            cum_t = jnp.concatenate([cum, pad], axis=0).T
        else:
            cum_t = cum.T
        xsc = xs[r0:r0 + q_len, :]
        xdt = _rowwise(lambda a, b: (a * b).astype(BF16), xsc, dt[r0:r0 + q_len, :], rows=VEC_ROWS)
        xdec = _rowwise(lambda a, b, cm_: (a * b * jnp.exp(cum_last - cm_)).astype(BF16),
                        xsc, dt[r0:r0 + q_len, :], cum, rows=VEC_ROWS)
        for g in range(di // SLAB):
            c0 = g * SLAB
            bg = bm[r0:r0 + q_len, g * D_STATE:(g + 1) * D_STATE].astype(BF16)
            cg = cm[r0:r0 + q_len, g * D_STATE:(g + 1) * D_STATE].astype(BF16)
            cb = lax.dot_general(cg, bg, _NT, preferred_element_type=F32)
            xg = xdt[:, c0:c0 + SLAB]
            y = jnp.zeros((q_len, SLAB), F32)
            for hh in range(HEADS_PER_SLAB):
                lane0 = c0 + hh * HEAD_DIM
                seg = cum[:, lane0:lane0 + 1] - cum_t[lane0:lane0 + 1, 0:q_len]
                decay = jnp.where(tri, jnp.exp(seg), 0.0)
                y = y + _mm((cb * decay).astype(BF16), jnp.where(slab_head == hh, xg, zero_x))
            st_old = st_t[:, c0:c0 + SLAB]
            y = y + _mm(cg, st_old.astype(BF16)) * ecum[:, c0:c0 + SLAB]
            st_t[:, c0:c0 + SLAB] = (st_old * elast[:, c0:c0 + SLAB]
                                     + lax.dot_general(bg, xdec[:, c0:c0 + SLAB], _TN,
                                                       preferred_element_type=F32))
            def gate_norm(y_, x_, z_, c0=c0):
                y_ = (y_ + dsk[:, c0:c0 + SLAB] * x_) * z_
                ms = jnp.mean(y_ * y_, axis=-1, keepdims=True)
                return (y_ * lax.rsqrt(ms + RMS_EPS) * nrm[:, c0:c0 + SLAB]).astype(BF16)

            mix[r0:r0 + q_len, c + c0:c + c0 + SLAB] = _rowwise(
                gate_norm, y, xsc[:, c0:c0 + SLAB], zg[r0:r0 + q_len, c0:c0 + SLAB], rows=VEC_ROWS)
    sto_ref[0] = st_t[...]

    h_ref[0] = _mm(mix[...], wout_ref[...]) + x


def _attn_bias(table, tq):
    heads = table.shape[0]
    nk = C_BAND + tq
    span = nk + tq - 1
    n_flat = C_BAND - C_MAX_REL + tq
    low = C_MAX_REL + 1 - tq
    assert low >= 0 and n_flat + 2 * C_MAX_REL - low == span
    vec = jnp.concatenate([jnp.broadcast_to(table[:, 2 * C_MAX_REL:], (heads, n_flat)),
                           table[:, low:2 * C_MAX_REL][:, ::-1]], axis=1)
    flat = jnp.tile(vec, (1, tq + 1))[:, :tq * (span + 1)]
    bias = flat.reshape(heads, tq, span + 1)[:, ::-1, :nk]
    qi = jnp.arange(tq)[:, None]
    kj = jnp.arange(nk)[None, :]
    dchunk = kj // CHUNK - qi // CHUNK
    ok = (dchunk >= 0) & (dchunk <= C_PREV_CHUNKS)
    bias = jnp.where(ok[None], bias * LOG2E, NEG_BIG).astype(F32)
    return bias.reshape(heads // HEADS_PER_SLAB, HEADS_PER_SLAB * tq, nk)


def _cd_layer(x, hist, norm_g, w_in, rel_bias, conv_w, conv_b, dt_bias, a_log, d_skip, norm_gd,
              w_out, *, tm):
    bsz, seq, d = x.shape
    heads = rel_bias.shape[0]
    c = heads * HEAD_DIM
    di = norm_gd.shape[0]
    d_heads = dt_bias.shape[0]
    xw = conv_w.shape[1]
    n_main = 3 * c + di + xw
    assert seq % tm == 0 and w_in.shape[1] == n_main + d_heads
    assert c % SLAB == 0 and di % SLAB == 0 and di // d_heads == HEAD_DIM
    tq = min(tm, ATTN_Q_BLOCK)
    q_len = min(tm, SSD_CHUNK)
    n_t = seq // tm
    keep = min(C_BAND, seq)
    assert tm == keep, "one tile must be exactly the K/V rows kept for the next call"
    has_hist = hist is not None

    rep = lambda p: jnp.repeat(p, HEAD_DIM).reshape(1, di)
    w_main = w_in[:, :n_main].astype(BF16)
    w_dt = jnp.repeat(w_in[:, n_main:], HEAD_DIM, axis=1).astype(BF16)
    bias = _attn_bias(rel_bias, tq)

    args = [x]
    in_specs = [pl.BlockSpec((1, tm, d), lambda b, t: (b, t, 0))]
    if has_hist:
        cache_k, cache_v, conv_rows, ssm = hist
        assert cache_k.shape[1] == C_BAND
        st_in = jnp.swapaxes(ssm.reshape(bsz, di, D_STATE), 1, 2)
        args += [cache_k.reshape(bsz, C_BAND, c), cache_v.reshape(bsz, C_BAND, c), conv_rows, st_in]
        in_specs += [pl.BlockSpec((1, C_BAND, c), lambda b, t: (b, 0, 0)),
                     pl.BlockSpec((1, C_BAND, c), lambda b, t: (b, 0, 0)),
                     pl.BlockSpec((1, 3, xw), lambda b, t: (b, 0, 0)),
                     pl.BlockSpec((1, D_STATE, di), lambda b, t: (b, 0, 0))]
    consts = [norm_g.reshape(1, d), w_main, w_dt, rep(dt_bias), rep(a_log), bias, conv_w,
              conv_b.reshape(1, xw), rep(d_skip), norm_gd.reshape(1, di), w_out.astype(BF16)]
    args += consts
    in_specs += [_const_spec(a.shape) for a in consts]

    out_shape = [jax.ShapeDtypeStruct((bsz, seq, d), F32),
                 jax.ShapeDtypeStruct((bsz, tm, c), F32),
                 jax.ShapeDtypeStruct((bsz, tm, c), F32),
                 jax.ShapeDtypeStruct((bsz, 3, xw), F32),
                 jax.ShapeDtypeStruct((bsz, D_STATE, di), F32)]
    out_specs = [pl.BlockSpec((1, tm, d), lambda b, t: (b, t, 0)),
                 pl.BlockSpec((1, tm, c), lambda b, t: (b, 0, 0)),
                 pl.BlockSpec((1, tm, c), lambda b, t: (b, 0, 0)),
                 pl.BlockSpec((1, 3, xw), lambda b, t: (b, 0, 0)),
                 pl.BlockSpec((1, D_STATE, di), lambda b, t: (b, 0, 0))]

    h, k_new, v_new, conv_new, st_new = pl.pallas_call(
        functools.partial(_cd_kernel, tm=tm, tq=tq, q_len=q_len, n_t=n_t, has_hist=has_hist),
        grid=(bsz, n_t),
        in_specs=in_specs,
        out_specs=out_specs,
        out_shape=out_shape,
        scratch_shapes=[pltpu.VMEM((2, C_BAND, c), BF16), pltpu.VMEM((2, C_BAND, c), BF16),
                        pltpu.VMEM((tm + 8, xw), F32), pltpu.VMEM((D_STATE, di), F32),
                        pltpu.VMEM((tm, c + di), BF16),
                        pltpu.VMEM((HEADS_PER_SLAB * tq, C_BAND + tq), BF16)],
        compiler_params=pltpu.CompilerParams(
            dimension_semantics=("parallel", "arbitrary"),
            vmem_limit_bytes=V7X_VMEM_LIMIT_BYTES),
        name="cd_mixer_hist" if has_hist else "cd_mixer",
    )(*args)
    k_new = k_new.reshape(bsz, tm, heads, HEAD_DIM)
    v_new = v_new.reshape(bsz, tm, heads, HEAD_DIM)
    st_new = jnp.swapaxes(st_new, 1, 2).reshape(bsz, d_heads, HEAD_DIM, D_STATE)
    return h, k_new, v_new, conv_new, st_new


def kernel(x_prompt, x_sample, cache_k_c, cache_v_c, state_conv_a, state_conv_d, state_ssm_d, norm_mix, norm_ffn, norm_final, w_in_ab, conv_w_a, ln_g_b, ln_b_b, w_s_b, b_s_b, w_out_ab, w_in_cd, rel_bias_c, conv_w_d, conv_b_d, dt_bias_d, a_log_d, d_skip_d, norm_g_d, w_out_cd, w_gate, w_up, w_down):
    bp, lp, _ = x_prompt.shape
    bs, ls, _ = x_sample.shape
    tp = min(PROMPT_TILE, lp)
    a_width = conv_w_a.shape[2]

    w_ab = (norm_mix[0], w_in_ab[0], conv_w_a[0], ln_g_b[0], ln_b_b[0], w_s_b[0], b_s_b[0],
            w_out_ab[0])
    hp, conv_a_p = _ab_layer(x_prompt, jnp.zeros((bp, 2, a_width), F32), *w_ab, tm=tp,
                             emit_v=False)
    hs, conv_a_s, v_b_s = _ab_layer(x_sample, state_conv_a[0], *w_ab, tm=ls, emit_v=True)
    w_f0 = (norm_ffn[0], w_gate[0], w_up[0], w_down[0])
    hp = _ffn_layer(hp, *w_f0, None, tm=tp)
    hs = _ffn_layer(hs, *w_f0, None, tm=min(bs * ls, PROMPT_TILE))

    w_cd = (norm_mix[1], w_in_cd[0], rel_bias_c[0], conv_w_d[0], conv_b_d[0], dt_bias_d[0],
            a_log_d[0], d_skip_d[0], norm_g_d[0], w_out_cd[0])
    hp, k_p, v_p, conv_d_p, ssm_p = _cd_layer(hp, None, *w_cd, tm=tp)
    hs, k_s, v_s, conv_d_s, ssm_s = _cd_layer(
        hs, (cache_k_c[0], cache_v_c[0], state_conv_d[0], state_ssm_d[0]), *w_cd, tm=ls)
    w_f1 = (norm_ffn[1], w_gate[1], w_up[1], w_down[1])
    y_p = _ffn_layer(hp, *w_f1, norm_final, tm=tp)
    y_s = _ffn_layer(hs, *w_f1, norm_final, tm=min(bs * ls, PROMPT_TILE))

    return (y_p, y_s, conv_a_p[None], conv_a_s[None], v_b_s[None], k_p[None], v_p[None],
            k_s[None], v_s[None], conv_d_p[None], conv_d_s[None], ssm_p[None], ssm_s[None])
```

```python
import functools

import jax
import jax.numpy as jnp
from jax import lax
from jax.experimental import pallas as pl
from jax.experimental.pallas import tpu as pltpu

F32 = jnp.float32
BF16 = jnp.bfloat16

RMS_EPS = 1e-5
LN_EPS = 1e-5
CHUNK = 64
C_PREV_CHUNKS = 8
C_BAND = C_PREV_CHUNKS * CHUNK
C_MAX_REL = 128
HEAD_DIM = 64
HEADS_PER_SLAB = 4
SLAB = HEAD_DIM * HEADS_PER_SLAB
B_GROUPS = 4
B_GROUP_DIM = 128
B_CHUNK = 128
D_STATE = 128
NEG_BIG = -1e30

V7X_VMEM_LIMIT_BYTES = 56 * 1024 * 1024
PROMPT_TILE = 512
SSD_CHUNK = 128
ATTN_Q_BLOCK = 128

_NT = (((1,), (1,)), ((), ()))
_TN = (((0,), (0,)), ((), ()))


def _const_spec(shape):
    nd = len(shape)
    return pl.BlockSpec(shape, lambda *_: (0,) * nd, pipeline_mode=pl.Buffered(1))


def _rms(x, g):
    return x * lax.rsqrt(jnp.mean(x * x, axis=-1, keepdims=True) + RMS_EPS) * g


def _gelu(x):
    return 0.5 * x * (1.0 + lax.erf(x * (2.0 ** -0.5)))


def _silu(x):
    hx = 0.5 * x
    return hx + hx * jnp.tanh(hx)


def _softplus(x):
    return jnp.maximum(x, 0.0) + jnp.log1p(jnp.exp(-jnp.abs(x)))


def _mm(a, b):
    return jnp.dot(a, b, preferred_element_type=F32)


def _ab_kernel(x_ref, hist_ref, ng_ref, win_ref, cw_ref, lng_ref, lnb_ref, ws_ref, bs_ref,
               wout_ref, *rest, tm, blk, width, emit_v):
    if emit_v:
        h_ref, nh_ref, v_ref, ubuf, mix = rest
    else:
        h_ref, nh_ref, ubuf, mix = rest
    a = width

    @pl.when(pl.program_id(1) == 0)
    def _():
        ubuf[0:8, :] = jnp.zeros((8, a), F32)
        ubuf[6:8, :] = hist_ref[0]

    x = x_ref[0]
    hn = _rms(x, ng_ref[...]).astype(BF16)

    def proj(i):
        return jnp.concatenate([_mm(hn, win_ref[:, i * a + j:i * a + j + SLAB])
                                for j in range(0, a, SLAB)], axis=1)

    v = proj(4)
    u = proj(3)
    vg = _gelu(v)
    mu = jnp.mean(vg, axis=-1, keepdims=True)
    vc = vg - mu
    var = jnp.mean(vc * vc, axis=-1, keepdims=True)
    vn = vc * lax.rsqrt(var + LN_EPS) * lng_ref[...] + lnb_ref[...]
    if emit_v:
        v_ref[0] = vn
    gate_c = proj(2)
    xa = proj(0)
    ug = _gelu(u)
    gate_b = proj(1)

    ua = gate_c * xa
    ubuf[8:8 + tm, :] = ua
    cw = cw_ref[...]
    conv = cw[0:1] * ubuf[6:6 + tm, :] + cw[1:2] * ubuf[7:7 + tm, :] + cw[2:3] * ua
    mix[:, 0:a] = (gate_b * conv).astype(BF16)
    tail = ua[tm - 2:tm, :]
    ubuf[6:8, :] = tail
    nh_ref[0] = tail

    vnb = vn.astype(BF16)
    for n in range(tm // blk):
        r0 = n * blk
        for g in range(B_GROUPS):
            c0 = g * B_GROUP_DIM
            f = _mm(ws_ref[g], vnb[r0:r0 + blk, c0:c0 + B_GROUP_DIM]) + bs_ref[g]
            mix[r0:r0 + blk, a + c0:a + c0 + B_GROUP_DIM] = (
                ug[r0:r0 + blk, c0:c0 + B_GROUP_DIM] * f).astype(BF16)

    h_ref[0] = _mm(mix[...], wout_ref[...]) + x


def _ab_layer(x, hist, norm_g, w_in, conv_w, ln_g, ln_b, w_s, b_s, w_out, *, tm, emit_v):
    bsz, seq, d = x.shape
    a = conv_w.shape[1]
    blk = min(seq, B_CHUNK)
    assert seq % tm == 0 and tm % blk == 0 and w_in.shape[1] == 5 * a and a % SLAB == 0
    tri = jnp.tril(jnp.ones((blk, blk), bool))
    ws = jnp.where(tri[None], w_s[:, :blk, :blk], 0).astype(BF16)
    bs = jnp.broadcast_to(b_s[:, :blk, None], (B_GROUPS, blk, B_GROUP_DIM)).astype(F32)

    out_shape = [jax.ShapeDtypeStruct((bsz, seq, d), F32),
                 jax.ShapeDtypeStruct((bsz, 2, a), F32)]
    out_specs = [pl.BlockSpec((1, tm, d), lambda b, t: (b, t, 0)),
                 pl.BlockSpec((1, 2, a), lambda b, t: (b, 0, 0))]
    if emit_v:
        out_shape.append(jax.ShapeDtypeStruct((bsz, seq, a), F32))
        out_specs.append(pl.BlockSpec((1, tm, a), lambda b, t: (b, t, 0)))

    return pl.pallas_call(
        functools.partial(_ab_kernel, tm=tm, blk=blk, width=a, emit_v=emit_v),
        grid=(bsz, seq // tm),
        in_specs=[
            pl.BlockSpec((1, tm, d), lambda b, t: (b, t, 0)),
            pl.BlockSpec((1, 2, a), lambda b, t: (b, 0, 0)),
            _const_spec((1, d)),
            _const_spec(w_in.shape),
            _const_spec(conv_w.shape),
            _const_spec((1, a)),
            _const_spec((1, a)),
            _const_spec(ws.shape),
            _const_spec(bs.shape),
            _const_spec(w_out.shape),
        ],
        out_specs=out_specs,
        out_shape=out_shape,
        scratch_shapes=[pltpu.VMEM((tm + 8, a), F32), pltpu.VMEM((tm, 2 * a), BF16)],
        compiler_params=pltpu.CompilerParams(
            dimension_semantics=("parallel", "arbitrary"),
            vmem_limit_bytes=V7X_VMEM_LIMIT_BYTES),
        name="ab_mixer",
    )(x, hist, norm_g.reshape(1, d), w_in.astype(BF16), conv_w, ln_g.reshape(1, a),
      ln_b.reshape(1, a), ws, bs, w_out.astype(BF16))


def _ffn_kernel(h_ref, ng_ref, wg_ref, wu_ref, wd_ref, *rest, final):
    if final:
        fg_ref, o_ref = rest
    else:
        (o_ref,) = rest
    h = h_ref[...]
    hn = _rms(h, ng_ref[...]).astype(BF16)
    gate = _mm(hn, wg_ref[...])
    up = _mm(hn, wu_ref[...])
    act = (_silu(gate) * up).astype(BF16)
    o = h + _mm(act, wd_ref[...])
    if final:
        o = _rms(o, fg_ref[...])
    o_ref[...] = o


def _ffn_layer(h, norm_g, w_gate, w_up, w_down, layer, final_g, *, tm):
    shape = h.shape
    d = shape[-1]
    rows = h.size // d
    assert rows % tm == 0
    final = final_g is not None

    def layer_spec(w):
        return pl.BlockSpec((None,) + w.shape[1:], lambda i: (layer, 0, 0),
                            pipeline_mode=pl.Buffered(1))

    args = [h.reshape(rows, d), norm_g.reshape(1, d), w_gate, w_up, w_down]
    in_specs = [pl.BlockSpec((tm, d), lambda i: (i, 0)), _const_spec((1, d)),
                layer_spec(w_gate), layer_spec(w_up), layer_spec(w_down)]
    if final:
        args.append(final_g.reshape(1, d))
        in_specs.append(_const_spec((1, d)))
    out = pl.pallas_call(
        functools.partial(_ffn_kernel, final=final),
        grid=(rows // tm,),
        in_specs=in_specs,
        out_specs=pl.BlockSpec((tm, d), lambda i: (i, 0)),
        out_shape=jax.ShapeDtypeStruct((rows, d), F32),
        compiler_params=pltpu.CompilerParams(
            dimension_semantics=("parallel",),
            vmem_limit_bytes=V7X_VMEM_LIMIT_BYTES),
        name="ffn_final" if final else "ffn",
    )(*args)
    return out.reshape(shape)


def _cd_kernel(*refs, tm, tq, q_len, n_t, has_hist):
    refs = list(refs)
    x_ref = refs.pop(0)
    if has_hist:
        kc_ref, vc_ref, ch_ref, st_ref = refs[:4]
        refs = refs[4:]
    (ng_ref, win_ref, wdt_ref, dtb_ref, alog_ref, bias_ref, cw_ref, cb_ref, dsk_ref, nrm_ref,
     wout_ref, h_ref, ko_ref, vo_ref, cho_ref, sto_ref, kbuf, vbuf, xbuf, st_t, mix) = refs
    t = pl.program_id(1)
    cw_dim = kbuf.shape[2]
    di = dsk_ref.shape[1]
    gn = D_STATE * (di // SLAB)
    xw = xbuf.shape[1]
    cur = lax.rem(t, 2)
    prev = 1 - cur

    @pl.when(t == 0)
    def _():
        xbuf[0:8, :] = jnp.zeros((8, xw), F32)
        if has_hist:
            kbuf[1] = kc_ref[0].astype(BF16)
            vbuf[1] = vc_ref[0].astype(BF16)
            xbuf[5:8, :] = ch_ref[0]
            st_t[...] = st_ref[0]
        else:
            kbuf[1] = jnp.zeros((C_BAND, cw_dim), BF16)
            vbuf[1] = jnp.zeros((C_BAND, cw_dim), BF16)
            st_t[...] = jnp.zeros(st_t.shape, F32)

    x = x_ref[0]
    hn = _rms(x, ng_ref[...]).astype(BF16)
    c = cw_dim
    z0, x0 = 3 * c, 3 * c + di

    def proj(col, w_ref=win_ref):
        return _mm(hn, w_ref[:, col:col + SLAB])

    for j in range(xw // SLAB):
        xbuf[8:8 + tm, j * SLAB:(j + 1) * SLAB] = proj(x0 + j * SLAB)
    tail = xbuf[tm + 5:tm + 8, :]
    cho_ref[0] = tail
    cw = cw_ref[...]
    cb_row = cb_ref[...]

    def conv_block(j):
        j0 = j * SLAB
        conv = (cw[0:1, j0:j0 + SLAB] * xbuf[5:5 + tm, j0:j0 + SLAB]
                + cw[1:2, j0:j0 + SLAB] * xbuf[6:6 + tm, j0:j0 + SLAB]
                + cw[2:3, j0:j0 + SLAB] * xbuf[7:7 + tm, j0:j0 + SLAB]
                + cw[3:4, j0:j0 + SLAB] * xbuf[8:8 + tm, j0:j0 + SLAB] + cb_row[:, j0:j0 + SLAB])
        return _silu(conv)

    n_slabs = di // SLAB
    assert xw // SLAB == 2 * n_slabs and c // SLAB == n_slabs
    dt_raw = [proj(g * SLAB, wdt_ref) for g in range(n_slabs)]
    xcs = [conv_block(j) for j in range(n_slabs)]
    z_raw = [proj(z0 + g * SLAB) for g in range(n_slabs)]
    xcs += [conv_block(n_slabs + j) for j in range(n_slabs)]
    xbuf[5:8, :] = tail
    q = jnp.concatenate([proj(g * SLAB) for g in range(n_slabs)], axis=1)
    dtb = dtb_ref[...]
    dt = jnp.concatenate([_softplus(dt_raw[g] + dtb[:, g * SLAB:(g + 1) * SLAB])
                          for g in range(n_slabs)], axis=1)
    k = jnp.concatenate([proj(c + g * SLAB) for g in range(n_slabs)], axis=1)
    zg = jnp.concatenate([_silu(zr) for zr in z_raw], axis=1)
    v = jnp.concatenate([proj(2 * c + g * SLAB) for g in range(n_slabs)], axis=1)
    q = (q * (HEAD_DIM ** -0.5)).astype(BF16)
    ko_ref[0] = k
    vo_ref[0] = v
    kbuf[cur, 0:tm, :] = k.astype(BF16)
    vbuf[cur, 0:tm, :] = v.astype(BF16)

    slab_head = lax.broadcasted_iota(jnp.int32, (1, SLAB), 1) // HEAD_DIM
    zero_q = jnp.zeros((tq, SLAB), BF16)
    for qb in range(tm // tq):
        qs = qb * tq
        n_prev = C_BAND - qs
        n_cur = qs + tq
        for s_i in range(c // SLAB):
            c0 = s_i * SLAB
            qg = q[qs:qs + tq, c0:c0 + SLAB]
            qst = jnp.concatenate([jnp.where(slab_head == hh, qg, zero_q)
                                   for hh in range(HEADS_PER_SLAB)], axis=0)
            kp = kbuf[prev, qs:C_BAND, c0:c0 + SLAB]
            kc = kbuf[cur, 0:n_cur, c0:c0 + SLAB]
            sp = lax.dot_general(qst, kp, _NT, preferred_element_type=F32)
            sc = lax.dot_general(qst, kc, _NT, preferred_element_type=F32)
            if not has_hist:
                sp = jnp.where(t > 0, sp, NEG_BIG)
            s = jnp.concatenate([sp, sc], axis=1) + bias_ref[s_i]
            p = jnp.exp(s - jnp.max(s, axis=-1, keepdims=True))
            l = jnp.sum(p, axis=-1, keepdims=True)
            pb = p.astype(BF16)
            pv = (_mm(pb[:, 0:n_prev], vbuf[prev, qs:C_BAND, c0:c0 + SLAB])
                  + _mm(pb[:, n_prev:], vbuf[cur, 0:n_cur, c0:c0 + SLAB])) * (1.0 / l)
            out = pv[0:tq]
            for hh in range(1, HEADS_PER_SLAB):
                out = jnp.where(slab_head == hh, pv[hh * tq:(hh + 1) * tq], out)
            mix[qs:qs + tq, c0:c0 + SLAB] = out.astype(BF16)

    xc = jnp.concatenate(xcs, axis=1)
    xs, bm, cm = xc[:, 0:di], xc[:, di:di + gn], xc[:, di + gn:di + 2 * gn]
    dta = dt * (-jnp.exp(alog_ref[...]))
    dsk = dsk_ref[...]
    nrm = nrm_ref[...]

    row_i = lax.broadcasted_iota(jnp.int32, (q_len, q_len), 0)
    col_i = lax.broadcasted_iota(jnp.int32, (q_len, q_len), 1)
    tri = row_i >= col_i
    tri_b = tri.astype(BF16)
    zero_x = jnp.zeros((q_len, SLAB), BF16)
    for ci in range(tm // q_len):
        r0 = ci * q_len
        dta_c = dta[r0:r0 + q_len, :]
        dta_hi = dta_c.astype(BF16)
        dta_lo = (dta_c - dta_hi.astype(F32)).astype(BF16)
        cum = _mm(tri_b, dta_hi) + _mm(tri_b, dta_lo)
        cum_last = cum[q_len - 1:q_len, :]
        ecum = jnp.exp(cum)
        dend = jnp.exp(cum_last - cum)
        elast = jnp.exp(cum_last)
        if q_len % 128:
            pad = jnp.zeros((128 - q_len % 128, di), F32)
            cum_t = jnp.concatenate([cum, pad], axis=0).T
        else:
            cum_t = cum.T
        xsc = xs[r0:r0 + q_len, :]
        xdt = xsc * dt[r0:r0 + q_len, :]
        xdec = (xdt * dend).astype(BF16)
        xdt = xdt.astype(BF16)
        for g in range(di // SLAB):
            c0 = g * SLAB
            bg = bm[r0:r0 + q_len, g * D_STATE:(g + 1) * D_STATE].astype(BF16)
            cg = cm[r0:r0 + q_len, g * D_STATE:(g + 1) * D_STATE].astype(BF16)
            cb = lax.dot_general(cg, bg, _NT, preferred_element_type=F32)
            xg = xdt[:, c0:c0 + SLAB]
            y = jnp.zeros((q_len, SLAB), F32)
            for hh in range(HEADS_PER_SLAB):
                lane0 = c0 + hh * HEAD_DIM
                seg = cum[:, lane0:lane0 + 1] - cum_t[lane0:lane0 + 1, 0:q_len]
                decay = jnp.where(tri, jnp.exp(seg), 0.0)
                y = y + _mm((cb * decay).astype(BF16), jnp.where(slab_head == hh, xg, zero_x))
            st_old = st_t[:, c0:c0 + SLAB]
            y = y + _mm(cg, st_old.astype(BF16)) * ecum[:, c0:c0 + SLAB]
            st_t[:, c0:c0 + SLAB] = (st_old * elast[:, c0:c0 + SLAB]
                                     + lax.dot_general(bg, xdec[:, c0:c0 + SLAB], _TN,
                                                       preferred_element_type=F32))
            y = (y + dsk[:, c0:c0 + SLAB] * xsc[:, c0:c0 + SLAB]) * zg[r0:r0 + q_len, c0:c0 + SLAB]
            ms = jnp.mean(y * y, axis=-1, keepdims=True)
            mix[r0:r0 + q_len, c + c0:c + c0 + SLAB] = (
                y * lax.rsqrt(ms + RMS_EPS) * nrm[:, c0:c0 + SLAB]).astype(BF16)
    sto_ref[0] = st_t[...]

    h_ref[0] = _mm(mix[...], wout_ref[...]) + x


def _attn_bias(table, tq):
    heads = table.shape[0]
    nk = C_BAND + tq
    span = nk + tq - 1
    n_flat = C_BAND - C_MAX_REL + tq
    low = C_MAX_REL + 1 - tq
    assert low >= 0 and n_flat + 2 * C_MAX_REL - low == span
    vec = jnp.concatenate([jnp.broadcast_to(table[:, 2 * C_MAX_REL:], (heads, n_flat)),
                           table[:, low:2 * C_MAX_REL][:, ::-1]], axis=1)
    flat = jnp.tile(vec, (1, tq + 1))[:, :tq * (span + 1)]
    bias = flat.reshape(heads, tq, span + 1)[:, ::-1, :nk]
    qi = jnp.arange(tq)[:, None]
    kj = jnp.arange(nk)[None, :]
    dchunk = kj // CHUNK - qi // CHUNK
    ok = (dchunk >= 0) & (dchunk <= C_PREV_CHUNKS)
    bias = jnp.where(ok[None], bias, NEG_BIG).astype(F32)
    return bias.reshape(heads // HEADS_PER_SLAB, HEADS_PER_SLAB * tq, nk)


def _cd_layer(x, hist, norm_g, w_in, rel_bias, conv_w, conv_b, dt_bias, a_log, d_skip, norm_gd,
              w_out, *, tm):
    bsz, seq, d = x.shape
    heads = rel_bias.shape[0]
    c = heads * HEAD_DIM
    di = norm_gd.shape[0]
    d_heads = dt_bias.shape[0]
    xw = conv_w.shape[1]
    n_main = 3 * c + di + xw
    assert seq % tm == 0 and w_in.shape[1] == n_main + d_heads
    assert c % SLAB == 0 and di % SLAB == 0 and di // d_heads == HEAD_DIM
    tq = min(tm, ATTN_Q_BLOCK)
    q_len = min(tm, SSD_CHUNK)
    n_t = seq // tm
    keep = min(C_BAND, seq)
    assert tm == keep, "one tile must be exactly the K/V rows kept for the next call"
    has_hist = hist is not None

    rep = lambda p: jnp.repeat(p, HEAD_DIM).reshape(1, di)
    w_main = w_in[:, :n_main].astype(BF16)
    w_dt = jnp.repeat(w_in[:, n_main:], HEAD_DIM, axis=1).astype(BF16)
    bias = _attn_bias(rel_bias, tq)

    args = [x]
    in_specs = [pl.BlockSpec((1, tm, d), lambda b, t: (b, t, 0))]
    if has_hist:
        cache_k, cache_v, conv_rows, ssm = hist
        assert cache_k.shape[1] == C_BAND
        st_in = jnp.swapaxes(ssm.reshape(bsz, di, D_STATE), 1, 2)
        args += [cache_k.reshape(bsz, C_BAND, c), cache_v.reshape(bsz, C_BAND, c), conv_rows, st_in]
        in_specs += [pl.BlockSpec((1, C_BAND, c), lambda b, t: (b, 0, 0)),
                     pl.BlockSpec((1, C_BAND, c), lambda b, t: (b, 0, 0)),
                     pl.BlockSpec((1, 3, xw), lambda b, t: (b, 0, 0)),
                     pl.BlockSpec((1, D_STATE, di), lambda b, t: (b, 0, 0))]
    consts = [norm_g.reshape(1, d), w_main, w_dt, rep(dt_bias), rep(a_log), bias, conv_w,
              conv_b.reshape(1, xw), rep(d_skip), norm_gd.reshape(1, di), w_out.astype(BF16)]
    args += consts
    in_specs += [_const_spec(a.shape) for a in consts]

    out_shape = [jax.ShapeDtypeStruct((bsz, seq, d), F32),
                 jax.ShapeDtypeStruct((bsz, tm, c), F32),
                 jax.ShapeDtypeStruct((bsz, tm, c), F32),
                 jax.ShapeDtypeStruct((bsz, 3, xw), F32),
                 jax.ShapeDtypeStruct((bsz, D_STATE, di), F32)]
    out_specs = [pl.BlockSpec((1, tm, d), lambda b, t: (b, t, 0)),
                 pl.BlockSpec((1, tm, c), lambda b, t: (b, 0, 0)),
                 pl.BlockSpec((1, tm, c), lambda b, t: (b, 0, 0)),
                 pl.BlockSpec((1, 3, xw), lambda b, t: (b, 0, 0)),
                 pl.BlockSpec((1, D_STATE, di), lambda b, t: (b, 0, 0))]

    h, k_new, v_new, conv_new, st_new = pl.pallas_call(
        functools.partial(_cd_kernel, tm=tm, tq=tq, q_len=q_len, n_t=n_t, has_hist=has_hist),
        grid=(bsz, n_t),
        in_specs=in_specs,
        out_specs=out_specs,
        out_shape=out_shape,
        scratch_shapes=[pltpu.VMEM((2, C_BAND, c), BF16), pltpu.VMEM((2, C_BAND, c), BF16),
                        pltpu.VMEM((tm + 8, xw), F32), pltpu.VMEM((D_STATE, di), F32),
                        pltpu.VMEM((tm, c + di), BF16)],
        compiler_params=pltpu.CompilerParams(
            dimension_semantics=("parallel", "arbitrary"),
            vmem_limit_bytes=V7X_VMEM_LIMIT_BYTES),
        name="cd_mixer_hist" if has_hist else "cd_mixer",
    )(*args)
    k_new = k_new.reshape(bsz, tm, heads, HEAD_DIM)
    v_new = v_new.reshape(bsz, tm, heads, HEAD_DIM)
    st_new = jnp.swapaxes(st_new, 1, 2).reshape(bsz, d_heads, HEAD_DIM, D_STATE)
    return h, k_new, v_new, conv_new, st_new


def kernel(x_prompt, x_sample, cache_k_c, cache_v_c, state_conv_a, state_conv_d, state_ssm_d, norm_mix, norm_ffn, norm_final, w_in_ab, conv_w_a, ln_g_b, ln_b_b, w_s_b, b_s_b, w_out_ab, w_in_cd, rel_bias_c, conv_w_d, conv_b_d, dt_bias_d, a_log_d, d_skip_d, norm_g_d, w_out_cd, w_gate, w_up, w_down):
    bp, lp, _ = x_prompt.shape
    bs, ls, _ = x_sample.shape
    tp = min(PROMPT_TILE, lp)
    a_width = conv_w_a.shape[2]

    w_ab = (norm_mix[0], w_in_ab[0], conv_w_a[0], ln_g_b[0], ln_b_b[0], w_s_b[0], b_s_b[0],
            w_out_ab[0])
    hp, conv_a_p = _ab_layer(x_prompt, jnp.zeros((bp, 2, a_width), F32), *w_ab, tm=tp,
                             emit_v=False)
    hs, conv_a_s, v_b_s = _ab_layer(x_sample, state_conv_a[0], *w_ab, tm=ls, emit_v=True)
    w_ffn = (w_gate.astype(BF16), w_up.astype(BF16), w_down.astype(BF16))
    hp = _ffn_layer(hp, norm_ffn[0], *w_ffn, 0, None, tm=tp)
    hs = _ffn_layer(hs, norm_ffn[0], *w_ffn, 0, None, tm=min(bs * ls, PROMPT_TILE))

    w_cd = (norm_mix[1], w_in_cd[0], rel_bias_c[0], conv_w_d[0], conv_b_d[0], dt_bias_d[0],
            a_log_d[0], d_skip_d[0], norm_g_d[0], w_out_cd[0])
    hp, k_p, v_p, conv_d_p, ssm_p = _cd_layer(hp, None, *w_cd, tm=tp)
    hs, k_s, v_s, conv_d_s, ssm_s = _cd_layer(
        hs, (cache_k_c[0], cache_v_c[0], state_conv_d[0], state_ssm_d[0]), *w_cd, tm=ls)
    y_p = _ffn_layer(hp, norm_ffn[1], *w_ffn, 1, norm_final, tm=tp)
    y_s = _ffn_layer(hs, norm_ffn[1], *w_ffn, 1, norm_final, tm=min(bs * ls, PROMPT_TILE))

    return (y_p, y_s, conv_a_p[None], conv_a_s[None], v_b_s[None], k_p[None], v_p[None],
            k_s[None], v_s[None], conv_d_p[None], conv_d_s[None], ssm_p[None], ssm_s[None])
```

```python
import functools

import jax
import jax.numpy as jnp
from jax import lax
from jax.experimental import pallas as pl
from jax.experimental.pallas import tpu as pltpu

F32 = jnp.float32
BF16 = jnp.bfloat16

RMS_EPS = 1e-5
LN_EPS = 1e-5
CHUNK = 64
C_PREV_CHUNKS = 8
C_BAND = C_PREV_CHUNKS * CHUNK
C_MAX_REL = 128
HEAD_DIM = 64
HEADS_PER_SLAB = 4
SLAB = HEAD_DIM * HEADS_PER_SLAB
B_GROUPS = 4
B_GROUP_DIM = 128
B_CHUNK = 128
D_STATE = 128
NEG_BIG = -1e30

V7X_VMEM_LIMIT_BYTES = 56 * 1024 * 1024
PROMPT_TILE = 512
AB_TILE = 1024
FFN_TILE = 1024
FFN_ROWS = 512
SSD_CHUNK = 128
ATTN_Q_BLOCK = 128

_NT = (((1,), (1,)), ((), ()))
_TN = (((0,), (0,)), ((), ()))


def _const_spec(shape):
    nd = len(shape)
    return pl.BlockSpec(shape, lambda *_: (0,) * nd, pipeline_mode=pl.Buffered(1))


def _rms(x, g):
    return x * lax.rsqrt(jnp.mean(x * x, axis=-1, keepdims=True) + RMS_EPS) * g


def _gelu(x):
    return 0.5 * x * (1.0 + lax.erf(x * (2.0 ** -0.5)))


def _silu(x):
    hx = 0.5 * x
    return hx + hx * jnp.tanh(hx)


def _softplus(x):
    return jnp.maximum(x, 0.0) + jnp.log1p(jnp.exp(-jnp.abs(x)))


def _mm(a, b):
    return jnp.dot(a, b, preferred_element_type=F32)


def _ab_kernel(x_ref, hist_ref, ng_ref, win_ref, cw_ref, lng_ref, lnb_ref, ws_ref, bs_ref,
               wout_ref, *rest, tm, blk, width, emit_v):
    if emit_v:
        h_ref, nh_ref, v_ref, ubuf, mix = rest
    else:
        h_ref, nh_ref, ubuf, mix = rest
    a = width

    @pl.when(pl.program_id(1) == 0)
    def _():
        ubuf[0:8, :] = jnp.zeros((8, a), F32)
        ubuf[6:8, :] = hist_ref[0]

    x = x_ref[0]
    hn = _rms(x, ng_ref[...]).astype(BF16)

    def proj(i):
        return jnp.concatenate([_mm(hn, win_ref[:, i * a + j:i * a + j + SLAB])
                                for j in range(0, a, SLAB)], axis=1)

    v = proj(4)
    u = proj(3)
    vg = _gelu(v)
    mu = jnp.mean(vg, axis=-1, keepdims=True)
    vc = vg - mu
    var = jnp.mean(vc * vc, axis=-1, keepdims=True)
    vn = vc * lax.rsqrt(var + LN_EPS) * lng_ref[...] + lnb_ref[...]
    if emit_v:
        v_ref[0] = vn
    gate_c = proj(2)
    xa = proj(0)
    ug = _gelu(u)
    gate_b = proj(1)

    ua = gate_c * xa
    ubuf[8:8 + tm, :] = ua
    cw = cw_ref[...]
    conv = cw[0:1] * ubuf[6:6 + tm, :] + cw[1:2] * ubuf[7:7 + tm, :] + cw[2:3] * ua
    mix[:, 0:a] = (gate_b * conv).astype(BF16)
    tail = ua[tm - 2:tm, :]
    ubuf[6:8, :] = tail
    nh_ref[0] = tail

    vnb = vn.astype(BF16)
    for n in range(tm // blk):
        r0 = n * blk
        for g in range(B_GROUPS):
            c0 = g * B_GROUP_DIM
            f = _mm(ws_ref[g], vnb[r0:r0 + blk, c0:c0 + B_GROUP_DIM]) + bs_ref[g]
            mix[r0:r0 + blk, a + c0:a + c0 + B_GROUP_DIM] = (
                ug[r0:r0 + blk, c0:c0 + B_GROUP_DIM] * f).astype(BF16)

    h_ref[0] = _mm(mix[...], wout_ref[...]) + x


def _ab_layer(x, hist, norm_g, w_in, conv_w, ln_g, ln_b, w_s, b_s, w_out, *, tm, emit_v):
    bsz, seq, d = x.shape
    a = conv_w.shape[1]
    blk = min(seq, B_CHUNK)
    assert seq % tm == 0 and tm % blk == 0 and w_in.shape[1] == 5 * a and a % SLAB == 0
    tri = jnp.tril(jnp.ones((blk, blk), bool))
    ws = jnp.where(tri[None], w_s[:, :blk, :blk], 0).astype(BF16)
    bs = jnp.broadcast_to(b_s[:, :blk, None], (B_GROUPS, blk, B_GROUP_DIM)).astype(F32)

    out_shape = [jax.ShapeDtypeStruct((bsz, seq, d), F32),
                 jax.ShapeDtypeStruct((bsz, 2, a), F32)]
    out_specs = [pl.BlockSpec((1, tm, d), lambda b, t: (b, t, 0)),
                 pl.BlockSpec((1, 2, a), lambda b, t: (b, 0, 0))]
    if emit_v:
        out_shape.append(jax.ShapeDtypeStruct((bsz, seq, a), F32))
        out_specs.append(pl.BlockSpec((1, tm, a), lambda b, t: (b, t, 0)))

    return pl.pallas_call(
        functools.partial(_ab_kernel, tm=tm, blk=blk, width=a, emit_v=emit_v),
        grid=(bsz, seq // tm),
        in_specs=[
            pl.BlockSpec((1, tm, d), lambda b, t: (b, t, 0)),
            pl.BlockSpec((1, 2, a), lambda b, t: (b, 0, 0)),
            _const_spec((1, d)),
            _const_spec(w_in.shape),
            _const_spec(conv_w.shape),
            _const_spec((1, a)),
            _const_spec((1, a)),
            _const_spec(ws.shape),
            _const_spec(bs.shape),
            _const_spec(w_out.shape),
        ],
        out_specs=out_specs,
        out_shape=out_shape,
        scratch_shapes=[pltpu.VMEM((tm + 8, a), F32), pltpu.VMEM((tm, 2 * a), BF16)],
        compiler_params=pltpu.CompilerParams(
            dimension_semantics=("parallel", "arbitrary"),
            vmem_limit_bytes=V7X_VMEM_LIMIT_BYTES),
        name="ab_mixer",
    )(x, hist, norm_g.reshape(1, d), w_in.astype(BF16), conv_w, ln_g.reshape(1, a),
      ln_b.reshape(1, a), ws, bs, w_out.astype(BF16))


def _ffn_kernel(h_ref, ng_ref, wg_ref, wu_ref, wd_ref, *rest, final):
    if final:
        fg_ref, o_ref = rest
    else:
        (o_ref,) = rest
    rows = min(FFN_ROWS, h_ref.shape[0])
    for r0 in range(0, h_ref.shape[0], rows):
        h = h_ref[r0:r0 + rows, :]
        hn = _rms(h, ng_ref[...]).astype(BF16)
        gate = _mm(hn, wg_ref[...])
        up = _mm(hn, wu_ref[...])
        act = (_silu(gate) * up).astype(BF16)
        o = h + _mm(act, wd_ref[...])
        if final:
            o = _rms(o, fg_ref[...])
        o_ref[r0:r0 + rows, :] = o


def _ffn_layer(h, norm_g, w_gate, w_up, w_down, layer, final_g, *, tm):
    shape = h.shape
    d = shape[-1]
    rows = h.size // d
    assert rows % tm == 0
    final = final_g is not None

    def layer_spec(w):
        return pl.BlockSpec((None,) + w.shape[1:], lambda i: (layer, 0, 0),
                            pipeline_mode=pl.Buffered(1))

    args = [h.reshape(rows, d), norm_g.reshape(1, d), w_gate, w_up, w_down]
    in_specs = [pl.BlockSpec((tm, d), lambda i: (i, 0)), _const_spec((1, d)),
                layer_spec(w_gate), layer_spec(w_up), layer_spec(w_down)]
    if final:
        args.append(final_g.reshape(1, d))
        in_specs.append(_const_spec((1, d)))
    out = pl.pallas_call(
        functools.partial(_ffn_kernel, final=final),
        grid=(rows // tm,),
        in_specs=in_specs,
        out_specs=pl.BlockSpec((tm, d), lambda i: (i, 0)),
        out_shape=jax.ShapeDtypeStruct((rows, d), F32),
        compiler_params=pltpu.CompilerParams(
            dimension_semantics=("parallel",),
            vmem_limit_bytes=V7X_VMEM_LIMIT_BYTES),
        name="ffn_final" if final else "ffn",
    )(*args)
    return out.reshape(shape)


def _cd_kernel(*refs, tm, tq, q_len, n_t, has_hist):
    refs = list(refs)
    x_ref = refs.pop(0)
    if has_hist:
        kc_ref, vc_ref, ch_ref, st_ref = refs[:4]
        refs = refs[4:]
    (ng_ref, win_ref, wdt_ref, dtb_ref, alog_ref, bias_ref, cw_ref, cb_ref, dsk_ref, nrm_ref,
     wout_ref, h_ref, ko_ref, vo_ref, cho_ref, sto_ref, kbuf, vbuf, xbuf, st_t, mix) = refs
    t = pl.program_id(1)
    cw_dim = kbuf.shape[2]
    di = dsk_ref.shape[1]
    gn = D_STATE * (di // SLAB)
    xw = xbuf.shape[1]
    cur = lax.rem(t, 2)
    prev = 1 - cur

    @pl.when(t == 0)
    def _():
        xbuf[0:8, :] = jnp.zeros((8, xw), F32)
        if has_hist:
            kbuf[1] = kc_ref[0].astype(BF16)
            vbuf[1] = vc_ref[0].astype(BF16)
            xbuf[5:8, :] = ch_ref[0]
            st_t[...] = st_ref[0]
        else:
            kbuf[1] = jnp.zeros((C_BAND, cw_dim), BF16)
            vbuf[1] = jnp.zeros((C_BAND, cw_dim), BF16)
            st_t[...] = jnp.zeros(st_t.shape, F32)

    x = x_ref[0]
    hn = _rms(x, ng_ref[...]).astype(BF16)
    c = cw_dim
    z0, x0 = 3 * c, 3 * c + di

    def proj(col, w_ref=win_ref):
        return _mm(hn, w_ref[:, col:col + SLAB])

    for j in range(xw // SLAB):
        xbuf[8:8 + tm, j * SLAB:(j + 1) * SLAB] = proj(x0 + j * SLAB)
    tail = xbuf[tm + 5:tm + 8, :]
    cho_ref[0] = tail
    cw = cw_ref[...]
    cb_row = cb_ref[...]

    def conv_block(j):
        j0 = j * SLAB
        conv = (cw[0:1, j0:j0 + SLAB] * xbuf[5:5 + tm, j0:j0 + SLAB]
                + cw[1:2, j0:j0 + SLAB] * xbuf[6:6 + tm, j0:j0 + SLAB]
                + cw[2:3, j0:j0 + SLAB] * xbuf[7:7 + tm, j0:j0 + SLAB]
                + cw[3:4, j0:j0 + SLAB] * xbuf[8:8 + tm, j0:j0 + SLAB] + cb_row[:, j0:j0 + SLAB])
        return _silu(conv)

    n_slabs = di // SLAB
    assert xw // SLAB == 2 * n_slabs and c // SLAB == n_slabs
    dt_raw = [proj(g * SLAB, wdt_ref) for g in range(n_slabs)]
    xcs = [conv_block(j) for j in range(n_slabs)]
    z_raw = [proj(z0 + g * SLAB) for g in range(n_slabs)]
    xcs += [conv_block(n_slabs + j) for j in range(n_slabs)]
    xbuf[5:8, :] = tail
    q = jnp.concatenate([proj(g * SLAB) for g in range(n_slabs)], axis=1)
    dtb = dtb_ref[...]
    dt = jnp.concatenate([_softplus(dt_raw[g] + dtb[:, g * SLAB:(g + 1) * SLAB])
                          for g in range(n_slabs)], axis=1)
    k = jnp.concatenate([proj(c + g * SLAB) for g in range(n_slabs)], axis=1)
    zg = jnp.concatenate([_silu(zr) for zr in z_raw], axis=1)
    v = jnp.concatenate([proj(2 * c + g * SLAB) for g in range(n_slabs)], axis=1)
    q = (q * (HEAD_DIM ** -0.5)).astype(BF16)
    ko_ref[0] = k
    vo_ref[0] = v
    kbuf[cur, 0:tm, :] = k.astype(BF16)
    vbuf[cur, 0:tm, :] = v.astype(BF16)

    slab_head = lax.broadcasted_iota(jnp.int32, (1, SLAB), 1) // HEAD_DIM
    zero_q = jnp.zeros((tq, SLAB), BF16)
    for qb in range(tm // tq):
        qs = qb * tq
        n_prev = C_BAND - qs
        n_cur = qs + tq
        for s_i in range(c // SLAB):
            c0 = s_i * SLAB
            qg = q[qs:qs + tq, c0:c0 + SLAB]
            qst = jnp.concatenate([jnp.where(slab_head == hh, qg, zero_q)
                                   for hh in range(HEADS_PER_SLAB)], axis=0)
            kp = kbuf[prev, qs:C_BAND, c0:c0 + SLAB]
            kc = kbuf[cur, 0:n_cur, c0:c0 + SLAB]
            sp = lax.dot_general(qst, kp, _NT, preferred_element_type=F32)
            sc = lax.dot_general(qst, kc, _NT, preferred_element_type=F32)
            if not has_hist:
                sp = jnp.where(t > 0, sp, NEG_BIG)
            s = jnp.concatenate([sp, sc], axis=1) + bias_ref[s_i]
            p = jnp.exp(s - jnp.max(s, axis=-1, keepdims=True))
            l = jnp.sum(p, axis=-1, keepdims=True)
            pb = p.astype(BF16)
            pv = (_mm(pb[:, 0:n_prev], vbuf[prev, qs:C_BAND, c0:c0 + SLAB])
                  + _mm(pb[:, n_prev:], vbuf[cur, 0:n_cur, c0:c0 + SLAB])) * (1.0 / l)
            out = pv[0:tq]
            for hh in range(1, HEADS_PER_SLAB):
                out = jnp.where(slab_head == hh, pv[hh * tq:(hh + 1) * tq], out)
            mix[qs:qs + tq, c0:c0 + SLAB] = out.astype(BF16)

    xc = jnp.concatenate(xcs, axis=1)
    xs, bm, cm = xc[:, 0:di], xc[:, di:di + gn], xc[:, di + gn:di + 2 * gn]
    dta = dt * (-jnp.exp(alog_ref[...]))
    dsk = dsk_ref[...]
    nrm = nrm_ref[...]

    row_i = lax.broadcasted_iota(jnp.int32, (q_len, q_len), 0)
    col_i = lax.broadcasted_iota(jnp.int32, (q_len, q_len), 1)
    tri = row_i >= col_i
    tri_b = tri.astype(BF16)
    zero_x = jnp.zeros((q_len, SLAB), BF16)
    for ci in range(tm // q_len):
        r0 = ci * q_len
        dta_c = dta[r0:r0 + q_len, :]
        dta_hi = dta_c.astype(BF16)
        dta_lo = (dta_c - dta_hi.astype(F32)).astype(BF16)
        cum = _mm(tri_b, dta_hi) + _mm(tri_b, dta_lo)
        cum_last = cum[q_len - 1:q_len, :]
        ecum = jnp.exp(cum)
        dend = jnp.exp(cum_last - cum)
        elast = jnp.exp(cum_last)
        if q_len % 128:
            pad = jnp.zeros((128 - q_len % 128, di), F32)
            cum_t = jnp.concatenate([cum, pad], axis=0).T
        else:
            cum_t = cum.T
        xsc = xs[r0:r0 + q_len, :]
        xdt = xsc * dt[r0:r0 + q_len, :]
        xdec = (xdt * dend).astype(BF16)
        xdt = xdt.astype(BF16)
        for g in range(di // SLAB):
            c0 = g * SLAB
            bg = bm[r0:r0 + q_len, g * D_STATE:(g + 1) * D_STATE].astype(BF16)
            cg = cm[r0:r0 + q_len, g * D_STATE:(g + 1) * D_STATE].astype(BF16)
            cb = lax.dot_general(cg, bg, _NT, preferred_element_type=F32)
            xg = xdt[:, c0:c0 + SLAB]
            y = jnp.zeros((q_len, SLAB), F32)
            for hh in range(HEADS_PER_SLAB):
                lane0 = c0 + hh * HEAD_DIM
                seg = cum[:, lane0:lane0 + 1] - cum_t[lane0:lane0 + 1, 0:q_len]
                decay = jnp.where(tri, jnp.exp(seg), 0.0)
                y = y + _mm((cb * decay).astype(BF16), jnp.where(slab_head == hh, xg, zero_x))
            st_old = st_t[:, c0:c0 + SLAB]
            y = y + _mm(cg, st_old.astype(BF16)) * ecum[:, c0:c0 + SLAB]
            st_t[:, c0:c0 + SLAB] = (st_old * elast[:, c0:c0 + SLAB]
                                     + lax.dot_general(bg, xdec[:, c0:c0 + SLAB], _TN,
                                                       preferred_element_type=F32))
            y = (y + dsk[:, c0:c0 + SLAB] * xsc[:, c0:c0 + SLAB]) * zg[r0:r0 + q_len, c0:c0 + SLAB]
            ms = jnp.mean(y * y, axis=-1, keepdims=True)
            mix[r0:r0 + q_len, c + c0:c + c0 + SLAB] = (
                y * lax.rsqrt(ms + RMS_EPS) * nrm[:, c0:c0 + SLAB]).astype(BF16)
    sto_ref[0] = st_t[...]

    h_ref[0] = _mm(mix[...], wout_ref[...]) + x


def _attn_bias(table, tq):
    heads = table.shape[0]
    nk = C_BAND + tq
    span = nk + tq - 1
    n_flat = C_BAND - C_MAX_REL + tq
    low = C_MAX_REL + 1 - tq
    assert low >= 0 and n_flat + 2 * C_MAX_REL - low == span
    vec = jnp.concatenate([jnp.broadcast_to(table[:, 2 * C_MAX_REL:], (heads, n_flat)),
                           table[:, low:2 * C_MAX_REL][:, ::-1]], axis=1)
    flat = jnp.tile(vec, (1, tq + 1))[:, :tq * (span + 1)]
    bias = flat.reshape(heads, tq, span + 1)[:, ::-1, :nk]
    qi = jnp.arange(tq)[:, None]
    kj = jnp.arange(nk)[None, :]
    dchunk = kj // CHUNK - qi // CHUNK
    ok = (dchunk >= 0) & (dchunk <= C_PREV_CHUNKS)
    bias = jnp.where(ok[None], bias, NEG_BIG).astype(F32)
    return bias.reshape(heads // HEADS_PER_SLAB, HEADS_PER_SLAB * tq, nk)


def _cd_layer(x, hist, norm_g, w_in, rel_bias, conv_w, conv_b, dt_bias, a_log, d_skip, norm_gd,
              w_out, *, tm):
    bsz, seq, d = x.shape
    heads = rel_bias.shape[0]
    c = heads * HEAD_DIM
    di = norm_gd.shape[0]
    d_heads = dt_bias.shape[0]
    xw = conv_w.shape[1]
    n_main = 3 * c + di + xw
    assert seq % tm == 0 and w_in.shape[1] == n_main + d_heads
    assert c % SLAB == 0 and di % SLAB == 0 and di // d_heads == HEAD_DIM
    tq = min(tm, ATTN_Q_BLOCK)
    q_len = min(tm, SSD_CHUNK)
    n_t = seq // tm
    keep = min(C_BAND, seq)
    assert tm == keep, "one tile must be exactly the K/V rows kept for the next call"
    has_hist = hist is not None

    rep = lambda p: jnp.repeat(p, HEAD_DIM).reshape(1, di)
    w_main = w_in[:, :n_main].astype(BF16)
    w_dt = jnp.repeat(w_in[:, n_main:], HEAD_DIM, axis=1).astype(BF16)
    bias = _attn_bias(rel_bias, tq)

    args = [x]
    in_specs = [pl.BlockSpec((1, tm, d), lambda b, t: (b, t, 0))]
    if has_hist:
        cache_k, cache_v, conv_rows, ssm = hist
        assert cache_k.shape[1] == C_BAND
        st_in = jnp.swapaxes(ssm.reshape(bsz, di, D_STATE), 1, 2)
        args += [cache_k.reshape(bsz, C_BAND, c), cache_v.reshape(bsz, C_BAND, c), conv_rows, st_in]
        in_specs += [pl.BlockSpec((1, C_BAND, c), lambda b, t: (b, 0, 0)),
                     pl.BlockSpec((1, C_BAND, c), lambda b, t: (b, 0, 0)),
                     pl.BlockSpec((1, 3, xw), lambda b, t: (b, 0, 0)),
                     pl.BlockSpec((1, D_STATE, di), lambda b, t: (b, 0, 0))]
    consts = [norm_g.reshape(1, d), w_main, w_dt, rep(dt_bias), rep(a_log), bias, conv_w,
              conv_b.reshape(1, xw), rep(d_skip), norm_gd.reshape(1, di), w_out.astype(BF16)]
    args += consts
    in_specs += [_const_spec(a.shape) for a in consts]

    out_shape = [jax.ShapeDtypeStruct((bsz, seq, d), F32),
                 jax.ShapeDtypeStruct((bsz, tm, c), F32),
                 jax.ShapeDtypeStruct((bsz, tm, c), F32),
                 jax.ShapeDtypeStruct((bsz, 3, xw), F32),
                 jax.ShapeDtypeStruct((bsz, D_STATE, di), F32)]
    out_specs = [pl.BlockSpec((1, tm, d), lambda b, t: (b, t, 0)),
                 pl.BlockSpec((1, tm, c), lambda b, t: (b, 0, 0)),
                 pl.BlockSpec((1, tm, c), lambda b, t: (b, 0, 0)),
                 pl.BlockSpec((1, 3, xw), lambda b, t: (b, 0, 0)),
                 pl.BlockSpec((1, D_STATE, di), lambda b, t: (b, 0, 0))]

    h, k_new, v_new, conv_new, st_new = pl.pallas_call(
        functools.partial(_cd_kernel, tm=tm, tq=tq, q_len=q_len, n_t=n_t, has_hist=has_hist),
        grid=(bsz, n_t),
        in_specs=in_specs,
        out_specs=out_specs,
        out_shape=out_shape,
        scratch_shapes=[pltpu.VMEM((2, C_BAND, c), BF16), pltpu.VMEM((2, C_BAND, c), BF16),
                        pltpu.VMEM((tm + 8, xw), F32), pltpu.VMEM((D_STATE, di), F32),
                        pltpu.VMEM((tm, c + di), BF16)],
        compiler_params=pltpu.CompilerParams(
            dimension_semantics=("parallel", "arbitrary"),
            vmem_limit_bytes=V7X_VMEM_LIMIT_BYTES),
        name="cd_mixer_hist" if has_hist else "cd_mixer",
    )(*args)
    k_new = k_new.reshape(bsz, tm, heads, HEAD_DIM)
    v_new = v_new.reshape(bsz, tm, heads, HEAD_DIM)
    st_new = jnp.swapaxes(st_new, 1, 2).reshape(bsz, d_heads, HEAD_DIM, D_STATE)
    return h, k_new, v_new, conv_new, st_new


def kernel(x_prompt, x_sample, cache_k_c, cache_v_c, state_conv_a, state_conv_d, state_ssm_d, norm_mix, norm_ffn, norm_final, w_in_ab, conv_w_a, ln_g_b, ln_b_b, w_s_b, b_s_b, w_out_ab, w_in_cd, rel_bias_c, conv_w_d, conv_b_d, dt_bias_d, a_log_d, d_skip_d, norm_g_d, w_out_cd, w_gate, w_up, w_down):
    bp, lp, _ = x_prompt.shape
    bs, ls, _ = x_sample.shape
    tp = min(PROMPT_TILE, lp)
    a_width = conv_w_a.shape[2]

    w_ab = (norm_mix[0], w_in_ab[0], conv_w_a[0], ln_g_b[0], ln_b_b[0], w_s_b[0], b_s_b[0],
            w_out_ab[0])
    ta = AB_TILE if lp % AB_TILE == 0 else tp
    hp, conv_a_p = _ab_layer(x_prompt, jnp.zeros((bp, 2, a_width), F32), *w_ab, tm=ta,
                             emit_v=False)
    hs, conv_a_s, v_b_s = _ab_layer(x_sample, state_conv_a[0], *w_ab, tm=ls, emit_v=True)
    w_ffn = (w_gate.astype(BF16), w_up.astype(BF16), w_down.astype(BF16))
    tf = FFN_TILE if (bp * lp) % FFN_TILE == 0 else tp
    hp = _ffn_layer(hp, norm_ffn[0], *w_ffn, 0, None, tm=tf)
    hs = _ffn_layer(hs, norm_ffn[0], *w_ffn, 0, None, tm=min(bs * ls, PROMPT_TILE))

    w_cd = (norm_mix[1], w_in_cd[0], rel_bias_c[0], conv_w_d[0], conv_b_d[0], dt_bias_d[0],
            a_log_d[0], d_skip_d[0], norm_g_d[0], w_out_cd[0])
    hp, k_p, v_p, conv_d_p, ssm_p = _cd_layer(hp, None, *w_cd, tm=tp)
    hs, k_s, v_s, conv_d_s, ssm_s = _cd_layer(
        hs, (cache_k_c[0], cache_v_c[0], state_conv_d[0], state_ssm_d[0]), *w_cd, tm=ls)
    y_p = _ffn_layer(hp, norm_ffn[1], *w_ffn, 1, norm_final, tm=tf)
    y_s = _ffn_layer(hs, norm_ffn[1], *w_ffn, 1, norm_final, tm=min(bs * ls, PROMPT_TILE))

    return (y_p, y_s, conv_a_p[None], conv_a_s[None], v_b_s[None], k_p[None], v_p[None],
            k_s[None], v_s[None], conv_d_p[None], conv_d_s[None], ssm_p[None], ssm_s[None])
```

```python
import functools

import jax
import jax.numpy as jnp
from jax import lax
from jax.experimental import pallas as pl
from jax.experimental.pallas import tpu as pltpu

F32 = jnp.float32
BF16 = jnp.bfloat16

RMS_EPS = 1e-5
LN_EPS = 1e-5
CHUNK = 64
C_PREV_CHUNKS = 8
C_BAND = C_PREV_CHUNKS * CHUNK
C_MAX_REL = 128
HEAD_DIM = 64
HEADS_PER_SLAB = 4
SLAB = HEAD_DIM * HEADS_PER_SLAB
B_GROUPS = 4
B_GROUP_DIM = 128
B_CHUNK = 128
D_STATE = 128
NEG_BIG = -1e30

V7X_VMEM_LIMIT_BYTES = 56 * 1024 * 1024
PROMPT_TILE = 512
AB_TILE = 1024
FFN_TILE = 1024
FFN_ROWS = 512
SSD_CHUNK = 128
ATTN_Q_BLOCK = 128
LOG2E = 1.4426950408889634

_NT = (((1,), (1,)), ((), ()))
_TN = (((0,), (0,)), ((), ()))


def _const_spec(shape):
    nd = len(shape)
    return pl.BlockSpec(shape, lambda *_: (0,) * nd, pipeline_mode=pl.Buffered(1))


def _rms(x, g):
    return x * lax.rsqrt(jnp.mean(x * x, axis=-1, keepdims=True) + RMS_EPS) * g


def _gelu(x):
    return 0.5 * x * (1.0 + lax.erf(x * (2.0 ** -0.5)))


def _silu(x):
    hx = 0.5 * x
    return hx + hx * jnp.tanh(hx)


def _softplus(x):
    return jnp.maximum(x, 0.0) + jnp.log1p(jnp.exp(-jnp.abs(x)))


def _mm(a, b):
    return jnp.dot(a, b, preferred_element_type=F32)


def _ab_kernel(x_ref, hist_ref, ng_ref, win_ref, cw_ref, lng_ref, lnb_ref, ws_ref, bs_ref,
               wout_ref, *rest, tm, blk, width, emit_v):
    if emit_v:
        h_ref, nh_ref, v_ref, ubuf, mix = rest
    else:
        h_ref, nh_ref, ubuf, mix = rest
    a = width

    @pl.when(pl.program_id(1) == 0)
    def _():
        ubuf[0:8, :] = jnp.zeros((8, a), F32)
        ubuf[6:8, :] = hist_ref[0]

    x = x_ref[0]
    hn = _rms(x, ng_ref[...]).astype(BF16)

    def proj(i):
        return jnp.concatenate([_mm(hn, win_ref[:, i * a + j:i * a + j + SLAB])
                                for j in range(0, a, SLAB)], axis=1)

    v = proj(4)
    u = proj(3)
    vg = _gelu(v)
    mu = jnp.mean(vg, axis=-1, keepdims=True)
    vc = vg - mu
    var = jnp.mean(vc * vc, axis=-1, keepdims=True)
    vn = vc * lax.rsqrt(var + LN_EPS) * lng_ref[...] + lnb_ref[...]
    if emit_v:
        v_ref[0] = vn
    gate_c = proj(2)
    xa = proj(0)
    ug = _gelu(u)
    gate_b = proj(1)

    ua = gate_c * xa
    ubuf[8:8 + tm, :] = ua
    cw = cw_ref[...]
    conv = cw[0:1] * ubuf[6:6 + tm, :] + cw[1:2] * ubuf[7:7 + tm, :] + cw[2:3] * ua
    mix[:, 0:a] = (gate_b * conv).astype(BF16)
    tail = ua[tm - 2:tm, :]
    ubuf[6:8, :] = tail
    nh_ref[0] = tail

    vnb = vn.astype(BF16)
    for n in range(tm // blk):
        r0 = n * blk
        for g in range(B_GROUPS):
            c0 = g * B_GROUP_DIM
            f = _mm(ws_ref[g], vnb[r0:r0 + blk, c0:c0 + B_GROUP_DIM]) + bs_ref[g]
            mix[r0:r0 + blk, a + c0:a + c0 + B_GROUP_DIM] = (
                ug[r0:r0 + blk, c0:c0 + B_GROUP_DIM] * f).astype(BF16)

    h_ref[0] = _mm(mix[...], wout_ref[...]) + x


def _ab_layer(x, hist, norm_g, w_in, conv_w, ln_g, ln_b, w_s, b_s, w_out, *, tm, emit_v):
    bsz, seq, d = x.shape
    a = conv_w.shape[1]
    blk = min(seq, B_CHUNK)
    assert seq % tm == 0 and tm % blk == 0 and w_in.shape[1] == 5 * a and a % SLAB == 0
    tri = jnp.tril(jnp.ones((blk, blk), bool))
    ws = jnp.where(tri[None], w_s[:, :blk, :blk], 0).astype(BF16)
    bs = jnp.broadcast_to(b_s[:, :blk, None], (B_GROUPS, blk, B_GROUP_DIM)).astype(F32)

    out_shape = [jax.ShapeDtypeStruct((bsz, seq, d), F32),
                 jax.ShapeDtypeStruct((bsz, 2, a), F32)]
    out_specs = [pl.BlockSpec((1, tm, d), lambda b, t: (b, t, 0)),
                 pl.BlockSpec((1, 2, a), lambda b, t: (b, 0, 0))]
    if emit_v:
        out_shape.append(jax.ShapeDtypeStruct((bsz, seq, a), F32))
        out_specs.append(pl.BlockSpec((1, tm, a), lambda b, t: (b, t, 0)))

    return pl.pallas_call(
        functools.partial(_ab_kernel, tm=tm, blk=blk, width=a, emit_v=emit_v),
        grid=(bsz, seq // tm),
        in_specs=[
            pl.BlockSpec((1, tm, d), lambda b, t: (b, t, 0)),
            pl.BlockSpec((1, 2, a), lambda b, t: (b, 0, 0)),
            _const_spec((1, d)),
            _const_spec(w_in.shape),
            _const_spec(conv_w.shape),
            _const_spec((1, a)),
            _const_spec((1, a)),
            _const_spec(ws.shape),
            _const_spec(bs.shape),
            _const_spec(w_out.shape),
        ],
        out_specs=out_specs,
        out_shape=out_shape,
        scratch_shapes=[pltpu.VMEM((tm + 8, a), F32), pltpu.VMEM((tm, 2 * a), BF16)],
        compiler_params=pltpu.CompilerParams(
            dimension_semantics=("parallel", "arbitrary"),
            vmem_limit_bytes=V7X_VMEM_LIMIT_BYTES),
        name="ab_mixer",
    )(x, hist, norm_g.reshape(1, d), w_in.astype(BF16), conv_w, ln_g.reshape(1, a),
      ln_b.reshape(1, a), ws, bs, w_out.astype(BF16))


def _ffn_kernel(hp_ref, hs_ref, ng_ref, wg_ref, wu_ref, wd_ref, *rest, final, n_p):
    if final:
        fg_ref, op_ref, os_ref = rest
    else:
        op_ref, os_ref = rest

    def run(h_ref, o_ref):
        rows = min(FFN_ROWS, h_ref.shape[0])
        for r0 in range(0, h_ref.shape[0], rows):
            h = h_ref[r0:r0 + rows, :]
            hn = _rms(h, ng_ref[...]).astype(BF16)
            gate = _mm(hn, wg_ref[...])
            up = _mm(hn, wu_ref[...])
            act = (_silu(gate) * up).astype(BF16)
            o = h + _mm(act, wd_ref[...])
            if final:
                o = _rms(o, fg_ref[...])
            o_ref[r0:r0 + rows, :] = o

    step = pl.program_id(0)
    pl.when(step < n_p)(functools.partial(run, hp_ref, op_ref))
    pl.when(step == n_p)(functools.partial(run, hs_ref, os_ref))


def _ffn_layer(hp, hs, norm_g, w_gate, w_up, w_down, layer, final_g, *, tm):
    d = hp.shape[-1]
    rows_p, rows_s = hp.size // d, hs.size // d
    assert rows_p % tm == 0
    n_p = rows_p // tm
    final = final_g is not None

    def layer_spec(w):
        return pl.BlockSpec((None,) + w.shape[1:], lambda i: (layer, 0, 0),
                            pipeline_mode=pl.Buffered(1))

    prompt_spec = pl.BlockSpec((tm, d), lambda i: (jnp.minimum(i, n_p - 1), 0))
    sample_spec = pl.BlockSpec((rows_s, d), lambda i: (0, 0))
    args = [hp.reshape(rows_p, d), hs.reshape(rows_s, d), norm_g.reshape(1, d), w_gate, w_up, w_down]
    in_specs = [prompt_spec, sample_spec, _const_spec((1, d)),
                layer_spec(w_gate), layer_spec(w_up), layer_spec(w_down)]
    if final:
        args.append(final_g.reshape(1, d))
        in_specs.append(_const_spec((1, d)))
    out_p, out_s = pl.pallas_call(
        functools.partial(_ffn_kernel, final=final, n_p=n_p),
        grid=(n_p + 1,),
        in_specs=in_specs,
        out_specs=[prompt_spec, sample_spec],
        out_shape=[jax.ShapeDtypeStruct((rows_p, d), F32), jax.ShapeDtypeStruct((rows_s, d), F32)],
        compiler_params=pltpu.CompilerParams(
            dimension_semantics=("arbitrary",),
            vmem_limit_bytes=V7X_VMEM_LIMIT_BYTES),
        name="ffn_final" if final else "ffn",
    )(*args)
    return out_p.reshape(hp.shape), out_s.reshape(hs.shape)


def _cd_kernel(*refs, tm, tq, q_len, n_t, has_hist):
    refs = list(refs)
    x_ref = refs.pop(0)
    if has_hist:
        kc_ref, vc_ref, ch_ref, st_ref = refs[:4]
        refs = refs[4:]
    (ng_ref, win_ref, wdt_ref, dtb_ref, alog_ref, bias_ref, cw_ref, cb_ref, dsk_ref, nrm_ref,
     wout_ref, h_ref, ko_ref, vo_ref, cho_ref, sto_ref, kbuf, vbuf, xbuf, st_t, mix) = refs
    t = pl.program_id(1)
    cw_dim = kbuf.shape[2]
    di = dsk_ref.shape[1]
    gn = D_STATE * (di // SLAB)
    xw = xbuf.shape[1]
    cur = lax.rem(t, 2)
    prev = 1 - cur

    @pl.when(t == 0)
    def _():
        xbuf[0:8, :] = jnp.zeros((8, xw), F32)
        if has_hist:
            kbuf[1] = kc_ref[0].astype(BF16)
            vbuf[1] = vc_ref[0].astype(BF16)
            xbuf[5:8, :] = ch_ref[0]
            st_t[...] = st_ref[0]
        else:
            kbuf[1] = jnp.zeros((C_BAND, cw_dim), BF16)
            vbuf[1] = jnp.zeros((C_BAND, cw_dim), BF16)
            st_t[...] = jnp.zeros(st_t.shape, F32)

    x = x_ref[0]
    hn = _rms(x, ng_ref[...]).astype(BF16)
    c = cw_dim
    z0, x0 = 3 * c, 3 * c + di

    def proj(col, w_ref=win_ref):
        return _mm(hn, w_ref[:, col:col + SLAB])

    for j in range(xw // SLAB):
        xbuf[8:8 + tm, j * SLAB:(j + 1) * SLAB] = proj(x0 + j * SLAB)
    tail = xbuf[tm + 5:tm + 8, :]
    cho_ref[0] = tail
    cw = cw_ref[...]
    cb_row = cb_ref[...]

    def conv_block(j):
        j0 = j * SLAB
        conv = (cw[0:1, j0:j0 + SLAB] * xbuf[5:5 + tm, j0:j0 + SLAB]
                + cw[1:2, j0:j0 + SLAB] * xbuf[6:6 + tm, j0:j0 + SLAB]
                + cw[2:3, j0:j0 + SLAB] * xbuf[7:7 + tm, j0:j0 + SLAB]
                + cw[3:4, j0:j0 + SLAB] * xbuf[8:8 + tm, j0:j0 + SLAB] + cb_row[:, j0:j0 + SLAB])
        return _silu(conv)

    n_slabs = di // SLAB
    assert xw // SLAB == 2 * n_slabs and c // SLAB == n_slabs
    dt_raw = [proj(g * SLAB, wdt_ref) for g in range(n_slabs)]
    xcs = [conv_block(j) for j in range(n_slabs)]
    z_raw = [proj(z0 + g * SLAB) for g in range(n_slabs)]
    xcs += [conv_block(n_slabs + j) for j in range(n_slabs)]
    xbuf[5:8, :] = tail
    q = jnp.concatenate([proj(g * SLAB) for g in range(n_slabs)], axis=1)
    dtb = dtb_ref[...]
    dt = jnp.concatenate([_softplus(dt_raw[g] + dtb[:, g * SLAB:(g + 1) * SLAB])
                          for g in range(n_slabs)], axis=1)
    k = jnp.concatenate([proj(c + g * SLAB) for g in range(n_slabs)], axis=1)
    zg = jnp.concatenate([_silu(zr) for zr in z_raw], axis=1)
    v = jnp.concatenate([proj(2 * c + g * SLAB) for g in range(n_slabs)], axis=1)
    q = (q * (HEAD_DIM ** -0.5 * LOG2E)).astype(BF16)
    ko_ref[0] = k
    vo_ref[0] = v
    kbuf[cur, 0:tm, :] = k.astype(BF16)
    vbuf[cur, 0:tm, :] = v.astype(BF16)

    slab_head = lax.broadcasted_iota(jnp.int32, (1, SLAB), 1) // HEAD_DIM
    zero_q = jnp.zeros((tq, SLAB), BF16)
    for qb in range(tm // tq):
        qs = qb * tq
        n_prev = C_BAND - qs
        n_cur = qs + tq
        for s_i in range(c // SLAB):
            c0 = s_i * SLAB
            qg = q[qs:qs + tq, c0:c0 + SLAB]
            qst = jnp.concatenate([jnp.where(slab_head == hh, qg, zero_q)
                                   for hh in range(HEADS_PER_SLAB)], axis=0)
            kp = kbuf[prev, qs:C_BAND, c0:c0 + SLAB]
            kc = kbuf[cur, 0:n_cur, c0:c0 + SLAB]
            sp = lax.dot_general(qst, kp, _NT, preferred_element_type=F32)
            sc = lax.dot_general(qst, kc, _NT, preferred_element_type=F32)
            if not has_hist:
                sp = jnp.where(t > 0, sp, NEG_BIG)
            s = jnp.concatenate([sp, sc], axis=1) + bias_ref[s_i]
            p = jnp.exp2(s - jnp.max(s, axis=-1, keepdims=True))
            l = jnp.sum(p, axis=-1, keepdims=True)
            pb = p.astype(BF16)
            pv = (_mm(pb[:, 0:n_prev], vbuf[prev, qs:C_BAND, c0:c0 + SLAB])
                  + _mm(pb[:, n_prev:], vbuf[cur, 0:n_cur, c0:c0 + SLAB])) * (1.0 / l)
            out = pv[0:tq]
            for hh in range(1, HEADS_PER_SLAB):
                out = jnp.where(slab_head == hh, pv[hh * tq:(hh + 1) * tq], out)
            mix[qs:qs + tq, c0:c0 + SLAB] = out.astype(BF16)

    xc = jnp.concatenate(xcs, axis=1)
    xs, bm, cm = xc[:, 0:di], xc[:, di:di + gn], xc[:, di + gn:di + 2 * gn]
    dta = dt * (-LOG2E * jnp.exp(alog_ref[...]))
    dsk = dsk_ref[...]
    nrm = nrm_ref[...]

    row_i = lax.broadcasted_iota(jnp.int32, (q_len, q_len), 0)
    col_i = lax.broadcasted_iota(jnp.int32, (q_len, q_len), 1)
    tri = row_i >= col_i
    tri_b = tri.astype(BF16)
    zero_x = jnp.zeros((q_len, SLAB), BF16)
    for ci in range(tm // q_len):
        r0 = ci * q_len
        dta_c = dta[r0:r0 + q_len, :]
        dta_hi = dta_c.astype(BF16)
        dta_lo = (dta_c - dta_hi.astype(F32)).astype(BF16)
        cum = _mm(tri_b, dta_hi) + _mm(tri_b, dta_lo)
        cum_last = cum[q_len - 1:q_len, :]
        ecum = jnp.exp2(cum)
        dend = jnp.exp2(cum_last - cum)
        elast = jnp.exp2(cum_last)
        if q_len % 128:
            pad = jnp.zeros((128 - q_len % 128, di), F32)
            cum_t = jnp.concatenate([cum, pad], axis=0).T
        else:
            cum_t = cum.T
        xsc = xs[r0:r0 + q_len, :]
        xdt = xsc * dt[r0:r0 + q_len, :]
        xdec = (xdt * dend).astype(BF16)
        xdt = xdt.astype(BF16)
        for g in range(di // SLAB):
            c0 = g * SLAB
            bg = bm[r0:r0 + q_len, g * D_STATE:(g + 1) * D_STATE].astype(BF16)
            cg = cm[r0:r0 + q_len, g * D_STATE:(g + 1) * D_STATE].astype(BF16)
            cb = lax.dot_general(cg, bg, _NT, preferred_element_type=F32)
            xg = xdt[:, c0:c0 + SLAB]
            y = jnp.zeros((q_len, SLAB), F32)
            for hh in range(HEADS_PER_SLAB):
                lane0 = c0 + hh * HEAD_DIM
                seg = cum[:, lane0:lane0 + 1] - cum_t[lane0:lane0 + 1, 0:q_len]
                decay = jnp.where(tri, jnp.exp2(seg), 0.0)
                y = y + _mm((cb * decay).astype(BF16), jnp.where(slab_head == hh, xg, zero_x))
            st_old = st_t[:, c0:c0 + SLAB]
            y = y + _mm(cg, st_old.astype(BF16)) * ecum[:, c0:c0 + SLAB]
            st_t[:, c0:c0 + SLAB] = (st_old * elast[:, c0:c0 + SLAB]
                                     + lax.dot_general(bg, xdec[:, c0:c0 + SLAB], _TN,
                                                       preferred_element_type=F32))
            y = (y + dsk[:, c0:c0 + SLAB] * xsc[:, c0:c0 + SLAB]) * zg[r0:r0 + q_len, c0:c0 + SLAB]
            ms = jnp.mean(y * y, axis=-1, keepdims=True)
            mix[r0:r0 + q_len, c + c0:c + c0 + SLAB] = (
                y * lax.rsqrt(ms + RMS_EPS) * nrm[:, c0:c0 + SLAB]).astype(BF16)
    sto_ref[0] = st_t[...]

    h_ref[0] = _mm(mix[...], wout_ref[...]) + x


def _attn_bias(table, tq):
    heads = table.shape[0]
    nk = C_BAND + tq
    span = nk + tq - 1
    n_flat = C_BAND - C_MAX_REL + tq
    low = C_MAX_REL + 1 - tq
    assert low >= 0 and n_flat + 2 * C_MAX_REL - low == span
    vec = jnp.concatenate([jnp.broadcast_to(table[:, 2 * C_MAX_REL:], (heads, n_flat)),
                           table[:, low:2 * C_MAX_REL][:, ::-1]], axis=1)
    flat = jnp.tile(vec, (1, tq + 1))[:, :tq * (span + 1)]
    bias = flat.reshape(heads, tq, span + 1)[:, ::-1, :nk]
    qi = jnp.arange(tq)[:, None]
    kj = jnp.arange(nk)[None, :]
    dchunk = kj // CHUNK - qi // CHUNK
    ok = (dchunk >= 0) & (dchunk <= C_PREV_CHUNKS)
    bias = jnp.where(ok[None], bias * LOG2E, NEG_BIG).astype(F32)
    return bias.reshape(heads // HEADS_PER_SLAB, HEADS_PER_SLAB * tq, nk)


def _cd_layer(x, hist, norm_g, w_in, rel_bias, conv_w, conv_b, dt_bias, a_log, d_skip, norm_gd,
              w_out, *, tm):
    bsz, seq, d = x.shape
    heads = rel_bias.shape[0]
    c = heads * HEAD_DIM
    di = norm_gd.shape[0]
    d_heads = dt_bias.shape[0]
    xw = conv_w.shape[1]
    n_main = 3 * c + di + xw
    assert seq % tm == 0 and w_in.shape[1] == n_main + d_heads
    assert c % SLAB == 0 and di % SLAB == 0 and di // d_heads == HEAD_DIM
    tq = min(tm, ATTN_Q_BLOCK)
    q_len = min(tm, SSD_CHUNK)
    n_t = seq // tm
    keep = min(C_BAND, seq)
    assert tm == keep, "one tile must be exactly the K/V rows kept for the next call"
    has_hist = hist is not None

    rep = lambda p: jnp.repeat(p, HEAD_DIM).reshape(1, di)
    w_main = w_in[:, :n_main].astype(BF16)
    w_dt = jnp.repeat(w_in[:, n_main:], HEAD_DIM, axis=1).astype(BF16)
    bias = _attn_bias(rel_bias, tq)

    args = [x]
    in_specs = [pl.BlockSpec((1, tm, d), lambda b, t: (b, t, 0))]
    if has_hist:
        cache_k, cache_v, conv_rows, ssm = hist
        assert cache_k.shape[1] == C_BAND
        st_in = jnp.swapaxes(ssm.reshape(bsz, di, D_STATE), 1, 2)
        args += [cache_k.reshape(bsz, C_BAND, c), cache_v.reshape(bsz, C_BAND, c), conv_rows, st_in]
        in_specs += [pl.BlockSpec((1, C_BAND, c), lambda b, t: (b, 0, 0)),
                     pl.BlockSpec((1, C_BAND, c), lambda b, t: (b, 0, 0)),
                     pl.BlockSpec((1, 3, xw), lambda b, t: (b, 0, 0)),
                     pl.BlockSpec((1, D_STATE, di), lambda b, t: (b, 0, 0))]
    consts = [norm_g.reshape(1, d), w_main, w_dt, rep(dt_bias), rep(a_log), bias, conv_w,
              conv_b.reshape(1, xw), rep(d_skip), norm_gd.reshape(1, di), w_out.astype(BF16)]
    args += consts
    in_specs += [_const_spec(a.shape) for a in consts]

    out_shape = [jax.ShapeDtypeStruct((bsz, seq, d), F32),
                 jax.ShapeDtypeStruct((bsz, tm, c), F32),
                 jax.ShapeDtypeStruct((bsz, tm, c), F32),
                 jax.ShapeDtypeStruct((bsz, 3, xw), F32),
                 jax.ShapeDtypeStruct((bsz, D_STATE, di), F32)]
    out_specs = [pl.BlockSpec((1, tm, d), lambda b, t: (b, t, 0)),
                 pl.BlockSpec((1, tm, c), lambda b, t: (b, 0, 0)),
                 pl.BlockSpec((1, tm, c), lambda b, t: (b, 0, 0)),
                 pl.BlockSpec((1, 3, xw), lambda b, t: (b, 0, 0)),
                 pl.BlockSpec((1, D_STATE, di), lambda b, t: (b, 0, 0))]

    h, k_new, v_new, conv_new, st_new = pl.pallas_call(
        functools.partial(_cd_kernel, tm=tm, tq=tq, q_len=q_len, n_t=n_t, has_hist=has_hist),
        grid=(bsz, n_t),
        in_specs=in_specs,
        out_specs=out_specs,
        out_shape=out_shape,
        scratch_shapes=[pltpu.VMEM((2, C_BAND, c), BF16), pltpu.VMEM((2, C_BAND, c), BF16),
                        pltpu.VMEM((tm + 8, xw), F32), pltpu.VMEM((D_STATE, di), F32),
                        pltpu.VMEM((tm, c + di), BF16)],
        compiler_params=pltpu.CompilerParams(
            dimension_semantics=("parallel", "arbitrary"),
            vmem_limit_bytes=V7X_VMEM_LIMIT_BYTES),
        name="cd_mixer_hist" if has_hist else "cd_mixer",
    )(*args)
    k_new = k_new.reshape(bsz, tm, heads, HEAD_DIM)
    v_new = v_new.reshape(bsz, tm, heads, HEAD_DIM)
    st_new = jnp.swapaxes(st_new, 1, 2).reshape(bsz, d_heads, HEAD_DIM, D_STATE)
    return h, k_new, v_new, conv_new, st_new


def kernel(x_prompt, x_sample, cache_k_c, cache_v_c, state_conv_a, state_conv_d, state_ssm_d, norm_mix, norm_ffn, norm_final, w_in_ab, conv_w_a, ln_g_b, ln_b_b, w_s_b, b_s_b, w_out_ab, w_in_cd, rel_bias_c, conv_w_d, conv_b_d, dt_bias_d, a_log_d, d_skip_d, norm_g_d, w_out_cd, w_gate, w_up, w_down):
    bp, lp, _ = x_prompt.shape
    bs, ls, _ = x_sample.shape
    tp = min(PROMPT_TILE, lp)
    a_width = conv_w_a.shape[2]

    w_ab = (norm_mix[0], w_in_ab[0], conv_w_a[0], ln_g_b[0], ln_b_b[0], w_s_b[0], b_s_b[0],
            w_out_ab[0])
    ta = AB_TILE if lp % AB_TILE == 0 else tp
    hp, conv_a_p = _ab_layer(x_prompt, jnp.zeros((bp, 2, a_width), F32), *w_ab, tm=ta,
                             emit_v=False)
    hs, conv_a_s, v_b_s = _ab_layer(x_sample, state_conv_a[0], *w_ab, tm=ls, emit_v=True)
    w_ffn = (w_gate.astype(BF16), w_up.astype(BF16), w_down.astype(BF16))
    tf = FFN_TILE if (bp * lp) % FFN_TILE == 0 else tp
    hp, hs = _ffn_layer(hp, hs, norm_ffn[0], *w_ffn, 0, None, tm=tf)

    w_cd = (norm_mix[1], w_in_cd[0], rel_bias_c[0], conv_w_d[0], conv_b_d[0], dt_bias_d[0],
            a_log_d[0], d_skip_d[0], norm_g_d[0], w_out_cd[0])
    hp, k_p, v_p, conv_d_p, ssm_p = _cd_layer(hp, None, *w_cd, tm=tp)
    hs, k_s, v_s, conv_d_s, ssm_s = _cd_layer(
        hs, (cache_k_c[0], cache_v_c[0], state_conv_d[0], state_ssm_d[0]), *w_cd, tm=ls)
    y_p, y_s = _ffn_layer(hp, hs, norm_ffn[1], *w_ffn, 1, norm_final, tm=tf)

    return (y_p, y_s, conv_a_p[None], conv_a_s[None], v_b_s[None], k_p[None], v_p[None],
            k_s[None], v_s[None], conv_d_p[None], conv_d_s[None], ssm_p[None], ssm_s[None])
```

```python
import functools

import jax
import jax.numpy as jnp
from jax import lax
from jax.experimental import pallas as pl
from jax.experimental.pallas import tpu as pltpu

F32 = jnp.float32
BF16 = jnp.bfloat16

RMS_EPS = 1e-5
LN_EPS = 1e-5
CHUNK = 64
C_PREV_CHUNKS = 8
C_BAND = C_PREV_CHUNKS * CHUNK
C_MAX_REL = 128
HEAD_DIM = 64
HEADS_PER_SLAB = 4
SLAB = HEAD_DIM * HEADS_PER_SLAB
B_GROUPS = 4
B_GROUP_DIM = 128
B_CHUNK = 128
D_STATE = 128
NEG_BIG = -1e30
V7X_SUBLANES = 8
V7X_LANES = 128

V7X_VMEM_LIMIT_BYTES = 56 * 1024 * 1024
PROMPT_TILE = 512
AB_TILE = 1024
FFN_TILE = 1024
FFN_ROWS = 512
SSD_CHUNK = 128
ATTN_Q_BLOCK = 128
LOG2E = 1.4426950408889634

_NT = (((1,), (1,)), ((), ()))
_TN = (((0,), (0,)), ((), ()))


def _const_spec(shape):
    nd = len(shape)
    return pl.BlockSpec(shape, lambda *_: (0,) * nd, pipeline_mode=pl.Buffered(1))


def _rms(x, g):
    return x * lax.rsqrt(jnp.mean(x * x, axis=-1, keepdims=True) + RMS_EPS) * g


def _gelu(x):
    return 0.5 * x * (1.0 + lax.erf(x * (2.0 ** -0.5)))


def _silu(x):
    hx = 0.5 * x
    return hx + hx * jnp.tanh(hx)


def _softplus(x):
    return jnp.maximum(x, 0.0) + jnp.log1p(jnp.exp(-jnp.abs(x)))


def _mm(a, b):
    return jnp.dot(a, b, preferred_element_type=F32)


def _ab_kernel(x_ref, hist_ref, ng_ref, win_ref, cw_ref, lng_ref, lnb_ref, ws_ref, bs_ref,
               wout_ref, *rest, tm, blk, width, emit_v):
    if emit_v:
        h_ref, nh_ref, v_ref, ubuf, mix = rest
    else:
        h_ref, nh_ref, ubuf, mix = rest
    a = width
    hdr = V7X_SUBLANES

    @pl.when(pl.program_id(1) == 0)
    def _():
        ubuf[0:hdr, :] = jnp.zeros((hdr, a), F32)
        ubuf[hdr - 2:hdr, :] = hist_ref[0]

    x = x_ref[0]
    hn = _rms(x, ng_ref[...]).astype(BF16)

    def proj(i):
        return jnp.concatenate([_mm(hn, win_ref[:, i * a + j:i * a + j + SLAB])
                                for j in range(0, a, SLAB)], axis=1)

    v = proj(4)
    u = proj(3)
    vg = _gelu(v)
    mu = jnp.mean(vg, axis=-1, keepdims=True)
    vc = vg - mu
    var = jnp.mean(vc * vc, axis=-1, keepdims=True)
    vn = vc * lax.rsqrt(var + LN_EPS) * lng_ref[...] + lnb_ref[...]
    if emit_v:
        v_ref[0] = vn
    gate_c = proj(2)
    xa = proj(0)
    ug = _gelu(u)
    gate_b = proj(1)

    ua = gate_c * xa
    ubuf[hdr:hdr + tm, :] = ua
    cw = cw_ref[...]
    conv = (cw[0:1] * ubuf[hdr - 2:hdr - 2 + tm, :] + cw[1:2] * ubuf[hdr - 1:hdr - 1 + tm, :]
            + cw[2:3] * ua)
    mix[:, 0:a] = (gate_b * conv).astype(BF16)
    tail = ua[tm - 2:tm, :]
    ubuf[hdr - 2:hdr, :] = tail
    nh_ref[0] = tail

    vnb = vn.astype(BF16)
    for n in range(tm // blk):
        r0 = n * blk
        for g in range(B_GROUPS):
            c0 = g * B_GROUP_DIM
            f = _mm(ws_ref[g], vnb[r0:r0 + blk, c0:c0 + B_GROUP_DIM]) + bs_ref[g]
            mix[r0:r0 + blk, a + c0:a + c0 + B_GROUP_DIM] = (
                ug[r0:r0 + blk, c0:c0 + B_GROUP_DIM] * f).astype(BF16)

    h_ref[0] = _mm(mix[...], wout_ref[...]) + x


def _ab_layer(x, hist, norm_g, w_in, conv_w, ln_g, ln_b, w_s, b_s, w_out, *, tm, emit_v):
    bsz, seq, d = x.shape
    a = conv_w.shape[1]
    blk = min(seq, B_CHUNK)
    assert seq % tm == 0 and tm % blk == 0 and w_in.shape[1] == 5 * a and a % SLAB == 0
    tri = jnp.tril(jnp.ones((blk, blk), bool))
    ws = jnp.where(tri[None], w_s[:, :blk, :blk], 0).astype(BF16)
    bs = jnp.broadcast_to(b_s[:, :blk, None], (B_GROUPS, blk, B_GROUP_DIM)).astype(F32)

    out_shape = [jax.ShapeDtypeStruct((bsz, seq, d), F32),
                 jax.ShapeDtypeStruct((bsz, 2, a), F32)]
    out_specs = [pl.BlockSpec((1, tm, d), lambda b, t: (b, t, 0)),
                 pl.BlockSpec((1, 2, a), lambda b, t: (b, 0, 0))]
    if emit_v:
        out_shape.append(jax.ShapeDtypeStruct((bsz, seq, a), F32))
        out_specs.append(pl.BlockSpec((1, tm, a), lambda b, t: (b, t, 0)))

    return pl.pallas_call(
        functools.partial(_ab_kernel, tm=tm, blk=blk, width=a, emit_v=emit_v),
        grid=(bsz, seq // tm),
        in_specs=[
            pl.BlockSpec((1, tm, d), lambda b, t: (b, t, 0)),
            pl.BlockSpec((1, 2, a), lambda b, t: (b, 0, 0)),
            _const_spec((1, d)),
            _const_spec(w_in.shape),
            _const_spec(conv_w.shape),
            _const_spec((1, a)),
            _const_spec((1, a)),
            _const_spec(ws.shape),
            _const_spec(bs.shape),
            _const_spec(w_out.shape),
        ],
        out_specs=out_specs,
        out_shape=out_shape,
        scratch_shapes=[pltpu.VMEM((tm + V7X_SUBLANES, a), F32), pltpu.VMEM((tm, 2 * a), BF16)],
        compiler_params=pltpu.CompilerParams(
            dimension_semantics=("parallel", "arbitrary"),
            vmem_limit_bytes=V7X_VMEM_LIMIT_BYTES),
        name="ab_mixer",
    )(x, hist, norm_g.reshape(1, d), w_in.astype(BF16), conv_w, ln_g.reshape(1, a),
      ln_b.reshape(1, a), ws, bs, w_out.astype(BF16))


def _ffn_kernel(hp_ref, hs_ref, ng_ref, wg_ref, wu_ref, wd_ref, *rest, final, n_p):
    if final:
        fg_ref, op_ref, os_ref = rest
    else:
        op_ref, os_ref = rest

    def run(h_ref, o_ref):
        rows = min(FFN_ROWS, h_ref.shape[0])
        for r0 in range(0, h_ref.shape[0], rows):
            h = h_ref[r0:r0 + rows, :]
            hn = _rms(h, ng_ref[...]).astype(BF16)
            gate = _mm(hn, wg_ref[...])
            up = _mm(hn, wu_ref[...])
            act = (_silu(gate) * up).astype(BF16)
            o = h + _mm(act, wd_ref[...])
            if final:
                o = _rms(o, fg_ref[...])
            o_ref[r0:r0 + rows, :] = o

    step = pl.program_id(0)
    pl.when(step < n_p)(functools.partial(run, hp_ref, op_ref))
    pl.when(step == n_p)(functools.partial(run, hs_ref, os_ref))


def _ffn_layer(hp, hs, norm_g, w_gate, w_up, w_down, layer, final_g, *, tm):
    d = hp.shape[-1]
    rows_p, rows_s = hp.size // d, hs.size // d
    assert rows_p % tm == 0
    n_p = rows_p // tm
    final = final_g is not None

    def layer_spec(w):
        return pl.BlockSpec((None,) + w.shape[1:], lambda i: (layer, 0, 0),
                            pipeline_mode=pl.Buffered(1))

    prompt_spec = pl.BlockSpec((tm, d), lambda i: (jnp.minimum(i, n_p - 1), 0))
    sample_spec = pl.BlockSpec((rows_s, d), lambda i: (0, 0))
    args = [hp.reshape(rows_p, d), hs.reshape(rows_s, d), norm_g.reshape(1, d), w_gate, w_up, w_down]
    in_specs = [prompt_spec, sample_spec, _const_spec((1, d)),
                layer_spec(w_gate), layer_spec(w_up), layer_spec(w_down)]
    if final:
        args.append(final_g.reshape(1, d))
        in_specs.append(_const_spec((1, d)))
    out_p, out_s = pl.pallas_call(
        functools.partial(_ffn_kernel, final=final, n_p=n_p),
        grid=(n_p + 1,),
        in_specs=in_specs,
        out_specs=[prompt_spec, sample_spec],
        out_shape=[jax.ShapeDtypeStruct((rows_p, d), F32), jax.ShapeDtypeStruct((rows_s, d), F32)],
        compiler_params=pltpu.CompilerParams(
            dimension_semantics=("arbitrary",),
            vmem_limit_bytes=V7X_VMEM_LIMIT_BYTES),
        name="ffn_final" if final else "ffn",
    )(*args)
    return out_p.reshape(hp.shape), out_s.reshape(hs.shape)


def _cd_kernel(*refs, tm, tq, q_len, n_t, has_hist):
    refs = list(refs)
    x_ref = refs.pop(0)
    if has_hist:
        kc_ref, vc_ref, ch_ref, st_ref = refs[:4]
        refs = refs[4:]
    (ng_ref, win_ref, wdt_ref, dtb_ref, alog_ref, bias_ref, cw_ref, cb_ref, dsk_ref, nrm_ref,
     wout_ref, h_ref, ko_ref, vo_ref, cho_ref, sto_ref, kbuf, vbuf, xbuf, st_t, mix) = refs
    t = pl.program_id(1)
    cw_dim = kbuf.shape[2]
    di = dsk_ref.shape[1]
    gn = D_STATE * (di // SLAB)
    xw = xbuf.shape[1]
    hdr = V7X_SUBLANES
    cur = lax.rem(t, 2)
    prev = 1 - cur

    @pl.when(t == 0)
    def _():
        xbuf[0:hdr, :] = jnp.zeros((hdr, xw), F32)
        if has_hist:
            kbuf[1] = kc_ref[0].astype(BF16)
            vbuf[1] = vc_ref[0].astype(BF16)
            xbuf[hdr - 3:hdr, :] = ch_ref[0]
            st_t[...] = st_ref[0]
        else:
            kbuf[1] = jnp.zeros((C_BAND, cw_dim), BF16)
            vbuf[1] = jnp.zeros((C_BAND, cw_dim), BF16)
            st_t[...] = jnp.zeros(st_t.shape, F32)

    x = x_ref[0]
    hn = _rms(x, ng_ref[...]).astype(BF16)
    c = cw_dim
    z0, x0 = 3 * c, 3 * c + di

    def proj(col, w_ref=win_ref):
        return _mm(hn, w_ref[:, col:col + SLAB])

    for j in range(xw // SLAB):
        xbuf[hdr:hdr + tm, j * SLAB:(j + 1) * SLAB] = proj(x0 + j * SLAB)
    tail = xbuf[tm + hdr - 3:tm + hdr, :]
    cho_ref[0] = tail
    cw = cw_ref[...]
    cb_row = cb_ref[...]

    def conv_block(j):
        j0 = j * SLAB
        taps = [cw[k:k + 1, j0:j0 + SLAB] * xbuf[hdr - 3 + k:hdr - 3 + k + tm, j0:j0 + SLAB]
                for k in range(4)]
        conv = taps[0] + taps[1] + taps[2] + taps[3] + cb_row[:, j0:j0 + SLAB]
        return _silu(conv)

    n_slabs = di // SLAB
    assert xw // SLAB == 2 * n_slabs and c // SLAB == n_slabs
    dt_raw = [proj(g * SLAB, wdt_ref) for g in range(n_slabs)]
    xcs = [conv_block(j) for j in range(n_slabs)]
    z_raw = [proj(z0 + g * SLAB) for g in range(n_slabs)]
    xcs += [conv_block(n_slabs + j) for j in range(n_slabs)]
    xbuf[hdr - 3:hdr, :] = tail
    q = jnp.concatenate([proj(g * SLAB) for g in range(n_slabs)], axis=1)
    dtb = dtb_ref[...]
    dt = jnp.concatenate([_softplus(dt_raw[g] + dtb[:, g * SLAB:(g + 1) * SLAB])
                          for g in range(n_slabs)], axis=1)
    k = jnp.concatenate([proj(c + g * SLAB) for g in range(n_slabs)], axis=1)
    zg = jnp.concatenate([_silu(zr) for zr in z_raw], axis=1)
    v = jnp.concatenate([proj(2 * c + g * SLAB) for g in range(n_slabs)], axis=1)
    q = (q * (HEAD_DIM ** -0.5 * LOG2E)).astype(BF16)
    ko_ref[0] = k
    vo_ref[0] = v
    kbuf[cur, 0:tm, :] = k.astype(BF16)
    vbuf[cur, 0:tm, :] = v.astype(BF16)

    slab_head = lax.broadcasted_iota(jnp.int32, (1, SLAB), 1) // HEAD_DIM
    zero_q = jnp.zeros((tq, SLAB), BF16)
    for qb in range(tm // tq):
        qs = qb * tq
        n_prev = C_BAND - qs
        n_cur = qs + tq
        for s_i in range(c // SLAB):
            c0 = s_i * SLAB
            qg = q[qs:qs + tq, c0:c0 + SLAB]
            qst = jnp.concatenate([jnp.where(slab_head == hh, qg, zero_q)
                                   for hh in range(HEADS_PER_SLAB)], axis=0)
            kp = kbuf[prev, qs:C_BAND, c0:c0 + SLAB]
            kc = kbuf[cur, 0:n_cur, c0:c0 + SLAB]
            sp = lax.dot_general(qst, kp, _NT, preferred_element_type=F32)
            sc = lax.dot_general(qst, kc, _NT, preferred_element_type=F32)
            if not has_hist:
                sp = jnp.where(t > 0, sp, NEG_BIG)
            s = jnp.concatenate([sp, sc], axis=1) + bias_ref[s_i]
            p = jnp.exp2(s - jnp.max(s, axis=-1, keepdims=True))
            l = jnp.sum(p, axis=-1, keepdims=True)
            pb = p.astype(BF16)
            pv = (_mm(pb[:, 0:n_prev], vbuf[prev, qs:C_BAND, c0:c0 + SLAB])
                  + _mm(pb[:, n_prev:], vbuf[cur, 0:n_cur, c0:c0 + SLAB])) * (1.0 / l)
            out = pv[0:tq]
            for hh in range(1, HEADS_PER_SLAB):
                out = jnp.where(slab_head == hh, pv[hh * tq:(hh + 1) * tq], out)
            mix[qs:qs + tq, c0:c0 + SLAB] = out.astype(BF16)

    xc = jnp.concatenate(xcs, axis=1)
    xs, bm, cm = xc[:, 0:di], xc[:, di:di + gn], xc[:, di + gn:di + 2 * gn]
    dta = dt * (-LOG2E * jnp.exp(alog_ref[...]))
    dsk = dsk_ref[...]
    nrm = nrm_ref[...]

    row_i = lax.broadcasted_iota(jnp.int32, (q_len, q_len), 0)
    col_i = lax.broadcasted_iota(jnp.int32, (q_len, q_len), 1)
    tri = row_i >= col_i
    tri_b = tri.astype(BF16)
    zero_x = jnp.zeros((q_len, SLAB), BF16)
    for ci in range(tm // q_len):
        r0 = ci * q_len
        dta_c = dta[r0:r0 + q_len, :]
        dta_hi = dta_c.astype(BF16)
        dta_lo = (dta_c - dta_hi.astype(F32)).astype(BF16)
        cum = _mm(tri_b, dta_hi) + _mm(tri_b, dta_lo)
        cum_last = cum[q_len - 1:q_len, :]
        ecum = jnp.exp2(cum)
        dend = jnp.exp2(cum_last - cum)
        elast = jnp.exp2(cum_last)
        if q_len % V7X_LANES:
            pad = jnp.zeros((V7X_LANES - q_len % V7X_LANES, di), F32)
            cum_t = jnp.concatenate([cum, pad], axis=0).T
        else:
            cum_t = cum.T
        xsc = xs[r0:r0 + q_len, :]
        xdt = xsc * dt[r0:r0 + q_len, :]
        xdec = (xdt * dend).astype(BF16)
        xdt = xdt.astype(BF16)
        for g in range(di // SLAB):
            c0 = g * SLAB
            bg = bm[r0:r0 + q_len, g * D_STATE:(g + 1) * D_STATE].astype(BF16)
            cg = cm[r0:r0 + q_len, g * D_STATE:(g + 1) * D_STATE].astype(BF16)
            cb = lax.dot_general(cg, bg, _NT, preferred_element_type=F32)
            xg = xdt[:, c0:c0 + SLAB]
            y = jnp.zeros((q_len, SLAB), F32)
            for hh in range(HEADS_PER_SLAB):
                lane0 = c0 + hh * HEAD_DIM
                seg = cum[:, lane0:lane0 + 1] - cum_t[lane0:lane0 + 1, 0:q_len]
                decay = jnp.where(tri, jnp.exp2(seg), 0.0)
                y = y + _mm((cb * decay).astype(BF16), jnp.where(slab_head == hh, xg, zero_x))
            st_old = st_t[:, c0:c0 + SLAB]
            y = y + _mm(cg, st_old.astype(BF16)) * ecum[:, c0:c0 + SLAB]
            st_t[:, c0:c0 + SLAB] = (st_old * elast[:, c0:c0 + SLAB]
                                     + lax.dot_general(bg, xdec[:, c0:c0 + SLAB], _TN,
                                                       preferred_element_type=F32))
            y = (y + dsk[:, c0:c0 + SLAB] * xsc[:, c0:c0 + SLAB]) * zg[r0:r0 + q_len, c0:c0 + SLAB]
            ms = jnp.mean(y * y, axis=-1, keepdims=True)
            mix[r0:r0 + q_len, c + c0:c + c0 + SLAB] = (
                y * lax.rsqrt(ms + RMS_EPS) * nrm[:, c0:c0 + SLAB]).astype(BF16)
    sto_ref[0] = st_t[...]

    h_ref[0] = _mm(mix[...], wout_ref[...]) + x


def _attn_bias(table, tq):
    heads = table.shape[0]
    nk = C_BAND + tq
    span = nk + tq - 1
    n_flat = C_BAND - C_MAX_REL + tq
    low = C_MAX_REL + 1 - tq
    assert low >= 0 and n_flat + 2 * C_MAX_REL - low == span
    vec = jnp.concatenate([jnp.broadcast_to(table[:, 2 * C_MAX_REL:], (heads, n_flat)),
                           table[:, low:2 * C_MAX_REL][:, ::-1]], axis=1)
    flat = jnp.tile(vec, (1, tq + 1))[:, :tq * (span + 1)]
    bias = flat.reshape(heads, tq, span + 1)[:, ::-1, :nk]
    qi = jnp.arange(tq)[:, None]
    kj = jnp.arange(nk)[None, :]
    dchunk = kj // CHUNK - qi // CHUNK
    ok = (dchunk >= 0) & (dchunk <= C_PREV_CHUNKS)
    bias = jnp.where(ok[None], bias * LOG2E, NEG_BIG).astype(F32)
    return bias.reshape(heads // HEADS_PER_SLAB, HEADS_PER_SLAB * tq, nk)


def _cd_layer(x, hist, norm_g, w_in, rel_bias, conv_w, conv_b, dt_bias, a_log, d_skip, norm_gd,
              w_out, *, tm):
    bsz, seq, d = x.shape
    heads = rel_bias.shape[0]
    c = heads * HEAD_DIM
    di = norm_gd.shape[0]
    d_heads = dt_bias.shape[0]
    xw = conv_w.shape[1]
    n_main = 3 * c + di + xw
    assert seq % tm == 0 and w_in.shape[1] == n_main + d_heads
    assert c % SLAB == 0 and di % SLAB == 0 and di // d_heads == HEAD_DIM
    tq = min(tm, ATTN_Q_BLOCK)
    q_len = min(tm, SSD_CHUNK)
    n_t = seq // tm
    keep = min(C_BAND, seq)
    assert tm == keep, "one tile must be exactly the K/V rows kept for the next call"
    has_hist = hist is not None

    rep = lambda p: jnp.repeat(p, HEAD_DIM).reshape(1, di)
    w_main = w_in[:, :n_main].astype(BF16)
    w_dt = jnp.repeat(w_in[:, n_main:], HEAD_DIM, axis=1).astype(BF16)
    bias = _attn_bias(rel_bias, tq)

    args = [x]
    in_specs = [pl.BlockSpec((1, tm, d), lambda b, t: (b, t, 0))]
    if has_hist:
        cache_k, cache_v, conv_rows, ssm = hist
        assert cache_k.shape[1] == C_BAND
        st_in = jnp.swapaxes(ssm.reshape(bsz, di, D_STATE), 1, 2)
        args += [cache_k.reshape(bsz, C_BAND, c), cache_v.reshape(bsz, C_BAND, c), conv_rows, st_in]
        in_specs += [pl.BlockSpec((1, C_BAND, c), lambda b, t: (b, 0, 0)),
                     pl.BlockSpec((1, C_BAND, c), lambda b, t: (b, 0, 0)),
                     pl.BlockSpec((1, 3, xw), lambda b, t: (b, 0, 0)),
                     pl.BlockSpec((1, D_STATE, di), lambda b, t: (b, 0, 0))]
    consts = [norm_g.reshape(1, d), w_main, w_dt, rep(dt_bias), rep(a_log), bias, conv_w,
              conv_b.reshape(1, xw), rep(d_skip), norm_gd.reshape(1, di), w_out.astype(BF16)]
    args += consts
    in_specs += [_const_spec(a.shape) for a in consts]

    out_shape = [jax.ShapeDtypeStruct((bsz, seq, d), F32),
                 jax.ShapeDtypeStruct((bsz, tm, c), F32),
                 jax.ShapeDtypeStruct((bsz, tm, c), F32),
                 jax.ShapeDtypeStruct((bsz, 3, xw), F32),
                 jax.ShapeDtypeStruct((bsz, D_STATE, di), F32)]
    out_specs = [pl.BlockSpec((1, tm, d), lambda b, t: (b, t, 0)),
                 pl.BlockSpec((1, tm, c), lambda b, t: (b, 0, 0)),
                 pl.BlockSpec((1, tm, c), lambda b, t: (b, 0, 0)),
                 pl.BlockSpec((1, 3, xw), lambda b, t: (b, 0, 0)),
                 pl.BlockSpec((1, D_STATE, di), lambda b, t: (b, 0, 0))]

    h, k_new, v_new, conv_new, st_new = pl.pallas_call(
        functools.partial(_cd_kernel, tm=tm, tq=tq, q_len=q_len, n_t=n_t, has_hist=has_hist),
        grid=(bsz, n_t),
        in_specs=in_specs,
        out_specs=out_specs,
        out_shape=out_shape,
        scratch_shapes=[pltpu.VMEM((2, C_BAND, c), BF16), pltpu.VMEM((2, C_BAND, c), BF16),
                        pltpu.VMEM((tm + V7X_SUBLANES, xw), F32), pltpu.VMEM((D_STATE, di), F32),
                        pltpu.VMEM((tm, c + di), BF16)],
        compiler_params=pltpu.CompilerParams(
            dimension_semantics=("parallel", "arbitrary"),
            vmem_limit_bytes=V7X_VMEM_LIMIT_BYTES),
        name="cd_mixer_hist" if has_hist else "cd_mixer",
    )(*args)
    k_new = k_new.reshape(bsz, tm, heads, HEAD_DIM)
    v_new = v_new.reshape(bsz, tm, heads, HEAD_DIM)
    st_new = jnp.swapaxes(st_new, 1, 2).reshape(bsz, d_heads, HEAD_DIM, D_STATE)
    return h, k_new, v_new, conv_new, st_new


def kernel(x_prompt, x_sample, cache_k_c, cache_v_c, state_conv_a, state_conv_d, state_ssm_d, norm_mix, norm_ffn, norm_final, w_in_ab, conv_w_a, ln_g_b, ln_b_b, w_s_b, b_s_b, w_out_ab, w_in_cd, rel_bias_c, conv_w_d, conv_b_d, dt_bias_d, a_log_d, d_skip_d, norm_g_d, w_out_cd, w_gate, w_up, w_down):
    bp, lp, _ = x_prompt.shape
    bs, ls, _ = x_sample.shape
    tp = min(PROMPT_TILE, lp)
    a_width = conv_w_a.shape[2]

    w_ab = (norm_mix[0], w_in_ab[0], conv_w_a[0], ln_g_b[0], ln_b_b[0], w_s_b[0], b_s_b[0],
            w_out_ab[0])
    ta = AB_TILE if lp % AB_TILE == 0 else tp
    hp, conv_a_p = _ab_layer(x_prompt, jnp.zeros((bp, 2, a_width), F32), *w_ab, tm=ta,
                             emit_v=False)
    hs, conv_a_s, v_b_s = _ab_layer(x_sample, state_conv_a[0], *w_ab, tm=ls, emit_v=True)
    w_ffn = (w_gate.astype(BF16), w_up.astype(BF16), w_down.astype(BF16))
    tf = FFN_TILE if (bp * lp) % FFN_TILE == 0 else tp
    hp, hs = _ffn_layer(hp, hs, norm_ffn[0], *w_ffn, 0, None, tm=tf)

    w_cd = (norm_mix[1], w_in_cd[0], rel_bias_c[0], conv_w_d[0], conv_b_d[0], dt_bias_d[0],
            a_log_d[0], d_skip_d[0], norm_g_d[0], w_out_cd[0])
    hp, k_p, v_p, conv_d_p, ssm_p = _cd_layer(hp, None, *w_cd, tm=tp)
    hs, k_s, v_s, conv_d_s, ssm_s = _cd_layer(
        hs, (cache_k_c[0], cache_v_c[0], state_conv_d[0], state_ssm_d[0]), *w_cd, tm=ls)
    y_p, y_s = _ffn_layer(hp, hs, norm_ffn[1], *w_ffn, 1, norm_final, tm=tf)

    return (y_p, y_s, conv_a_p[None], conv_a_s[None], v_b_s[None], k_p[None], v_p[None],
            k_s[None], v_s[None], conv_d_p[None], conv_d_s[None], ssm_p[None], ssm_s[None])
```
